```python
import math
import jax
import jax.numpy as jnp
from jax import lax
import numpy as np

D_MODEL = 1024
BATCH = 32
SEQ = 256
DEPTH = 1
DEC_BATCH = 8
DEC_SEQ = 4096
PAST_LEN = 512

GRID_W = 64
HEAD_DIM = 64
A_Q_HEADS = 8
A_KV_HEADS = 2
A_GROUP = A_Q_HEADS // A_KV_HEADS
B_HEADS = 4
B_V_DIM = 2 * HEAD_DIM
A_WIDTH = A_Q_HEADS * HEAD_DIM
B_WIDTH = B_HEADS * B_V_DIM
A_Q_COLS = A_Q_HEADS * HEAD_DIM
A_KV_COLS = A_KV_HEADS * HEAD_DIM
B_QK_COLS = B_HEADS * 2 * HEAD_DIM
B_V_COLS = B_HEADS * B_V_DIM
SPLIT_POINTS = (A_Q_COLS, A_Q_COLS + A_KV_COLS, A_Q_COLS + 2 * A_KV_COLS,
                A_Q_COLS + 2 * A_KV_COLS + B_QK_COLS, A_Q_COLS + 2 * A_KV_COLS + 2 * B_QK_COLS)
IN_COLS = A_Q_COLS + 2 * A_KV_COLS + 2 * B_QK_COLS + B_V_COLS
N_BRANCHES = 2
D_FF = 4 * D_MODEL
N_MOD = 6
Q_BLOCK = 128
ROPE_BASE = 10000.0
EPS = 1e-6

kernel_name = "hybrid_dit_gqa_diffattn_step"


def rms_norm(x, g):
    xf = x.astype(jnp.float32)
    y = xf * lax.rsqrt(jnp.mean(xf * xf, axis=-1, keepdims=True) + EPS)
    return (y * g.astype(jnp.float32)).astype(x.dtype)


def modulation(cond, w_mod, b_mod):
    m = jax.nn.silu(cond) @ w_mod + b_mod
    return jnp.split(m, N_MOD, axis=-1)


def axial_rope_tables(n_tokens, dtype):
    rows = n_tokens // GRID_W
    row = jnp.broadcast_to(jnp.arange(rows, dtype=jnp.float32)[:, None], (rows, GRID_W)).reshape(-1)
    col = jnp.broadcast_to(jnp.arange(GRID_W, dtype=jnp.float32)[None, :], (rows, GRID_W)).reshape(-1)
    n_freq = HEAD_DIM // 4
    freqs = ROPE_BASE ** (-jnp.arange(n_freq, dtype=jnp.float32) / n_freq)
    ang = jnp.concatenate([row[:, None] * freqs, col[:, None] * freqs], axis=-1)
    return jnp.cos(ang).astype(dtype), jnp.sin(ang).astype(dtype)


def apply_rope(x, cos, sin):
    half = HEAD_DIM // 2
    shape = (1, cos.shape[0]) + (1,) * (x.ndim - 3) + (half,)
    c = cos.reshape(shape)
    s = sin.reshape(shape)
    x1, x2 = x[..., :half], x[..., half:]
    return jnp.concatenate([x1 * c - x2 * s, x1 * s + x2 * c], axis=-1)


def sweep_query_blocks(fn, q):
    b, s = q.shape[:2]
    nb = s // Q_BLOCK
    blocks = jnp.moveaxis(q.reshape((b, nb, Q_BLOCK) + q.shape[2:]), 1, 0)
    out = lax.map(fn, blocks)
    out = jnp.moveaxis(out, 0, 1)
    return out.reshape((b, s) + out.shape[3:])


def gqa_attend(q, k, v):
    scale = HEAD_DIM ** -0.5

    def block(qb):
        qg = qb.reshape(qb.shape[:2] + (A_KV_HEADS, A_GROUP, HEAD_DIM))
        s = jnp.einsum('bqgrd,bkgd->bgrqk', qg, k, preferred_element_type=jnp.float32) * scale
        p = jax.nn.softmax(s, axis=-1).astype(v.dtype)
        o = jnp.einsum('bgrqk,bkgd->bqgrd', p, v)
        return o.reshape(qb.shape)

    return sweep_query_blocks(block, q)


def diff_attend(q, k, v, lam):
    scale = HEAD_DIM ** -0.5

    def block(qb):
        s = jnp.einsum('bqhcd,bkhcd->bhcqk', qb, k, preferred_element_type=jnp.float32) * scale
        p = jax.nn.softmax(s, axis=-1)
        w = (p[:, :, 0] - lam * p[:, :, 1]).astype(v.dtype)
        return jnp.einsum('bhqk,bkhe->bqhe', w, v)

    return sweep_query_blocks(block, q)


def trunk_layer(x, mods, lp, lam_init, rope=None, ctx_kv=None):
    shift1, scale1, gate1, shift2, scale2, gate2 = mods
    b, s, _ = x.shape
    h = rms_norm(x, lp['norm1_g']) * (1 + scale1) + shift1
    a_q, a_k, a_v, b_q, b_k, b_v = jnp.split(h @ lp['w_in'], SPLIT_POINTS, axis=-1)
    a_q = rms_norm(a_q.reshape(b, s, A_Q_HEADS, HEAD_DIM), lp['a_q_norm_g'])
    a_k = rms_norm(a_k.reshape(b, s, A_KV_HEADS, HEAD_DIM), lp['a_k_norm_g'])
    a_v = a_v.reshape(b, s, A_KV_HEADS, HEAD_DIM)
    b_q = b_q.reshape(b, s, B_HEADS, 2, HEAD_DIM)
    b_k = b_k.reshape(b, s, B_HEADS, 2, HEAD_DIM)
    b_v = b_v.reshape(b, s, B_HEADS, B_V_DIM)
    if rope is not None:
        cos, sin = rope
        a_q = apply_rope(a_q, cos, sin)
        a_k = apply_rope(a_k, cos, sin)
        b_q = apply_rope(b_q, cos, sin)
        b_k = apply_rope(b_k, cos, sin)
    own_kv = (a_k, a_v, b_k, b_v)
    if ctx_kv is None:
        keys = own_kv
    else:
        keys = tuple(jnp.concatenate([kc, ko], axis=1) for kc, ko in zip(ctx_kv, own_kv))
    f32 = jnp.float32
    lam = (jnp.exp(jnp.sum(lp['lam_q1'].astype(f32) * lp['lam_k1'].astype(f32)))
           - jnp.exp(jnp.sum(lp['lam_q2'].astype(f32) * lp['lam_k2'].astype(f32))) + lam_init)
    a_o = gqa_attend(a_q, keys[0], keys[1]).reshape(b, s, A_WIDTH)
    b_o = diff_attend(b_q, keys[2], keys[3], lam)
    b_o = (rms_norm(b_o, lp['b_subln_g']) * (1 - lam_init)).reshape(b, s, B_WIDTH)
    gate_a, gate_b = jnp.split(jax.nn.sigmoid(h @ lp['w_gate'] + lp['b_gate']), N_BRANCHES, axis=-1)
    merged = gate_a * (a_o @ lp['w_br_a']) + gate_b * (b_o @ lp['w_br_b'])
    x = x + gate1 * (merged @ lp['w_out'])
    h2 = rms_norm(x, lp['norm2_g']) * (1 + scale2) + shift2
    x = x + gate2 * (jnp.square(jax.nn.relu(h2 @ lp['w_fc1'])) @ lp['w_fc2'])
    return x, own_kv


def setup_inputs(seed: int = 0) -> dict:
    key = jax.random.key(seed)
    ks = iter(jax.random.split(key, 40))

    def nrm(shape, scale):
        return scale * jax.random.normal(next(ks), shape, jnp.float32)

    def gain(shape):
        return 1.0 + nrm(shape, 0.02)

    d = D_MODEL
    return {
        'x_prompt': nrm((BATCH, SEQ, d), 1.0),
        'x_sample': nrm((DEC_BATCH, DEC_SEQ, d), 1.0),
        'cache_a_k': nrm((DEC_BATCH, DEPTH, PAST_LEN, A_KV_HEADS, HEAD_DIM), 1.0),
        'cache_a_v': nrm((DEC_BATCH, DEPTH, PAST_LEN, A_KV_HEADS, HEAD_DIM), 1.0),
        'cache_b_k': nrm((DEC_BATCH, DEPTH, PAST_LEN, B_HEADS, 2, HEAD_DIM), 1.0),
        'cache_b_v': nrm((DEC_BATCH, DEPTH, PAST_LEN, B_HEADS, B_V_DIM), 1.0),
        'c': nrm((DEC_BATCH, d), 1.0),
        'c_ctx': nrm((d,), 1.0),
        'w_mod': nrm((DEPTH, d, N_MOD * d), d ** -0.5),
        'b_mod': nrm((DEPTH, N_MOD * d), 0.02),
        'norm1_g': gain((DEPTH, d)),
        'w_in': nrm((DEPTH, d, IN_COLS), d ** -0.5),
        'a_q_norm_g': gain((DEPTH, HEAD_DIM)),
        'a_k_norm_g': gain((DEPTH, HEAD_DIM)),
        'lam_q1': nrm((DEPTH, HEAD_DIM), 0.1),
        'lam_k1': nrm((DEPTH, HEAD_DIM), 0.1),
        'lam_q2': nrm((DEPTH, HEAD_DIM), 0.1),
        'lam_k2': nrm((DEPTH, HEAD_DIM), 0.1),
        'b_subln_g': gain((DEPTH, B_V_DIM)),
        'w_gate': nrm((DEPTH, d, N_BRANCHES * d), d ** -0.5),
        'b_gate': nrm((DEPTH, N_BRANCHES * d), 0.02),
        'w_br_a': nrm((DEPTH, A_WIDTH, d), A_WIDTH ** -0.5),
        'w_br_b': nrm((DEPTH, B_WIDTH, d), B_WIDTH ** -0.5),
        'w_out': nrm((DEPTH, d, d), d ** -0.5),
        'norm2_g': gain((DEPTH, d)),
        'w_fc1': nrm((DEPTH, d, D_FF), d ** -0.5),
        'w_fc2': nrm((DEPTH, D_FF, d), D_FF ** -0.5),
        'final_norm_g': gain((d,)),
    }


def reference(x_prompt, x_sample, cache_a_k, cache_a_v, cache_b_k, cache_b_v, c, c_ctx,
              w_mod, b_mod, norm1_g, w_in, a_q_norm_g, a_k_norm_g, lam_q1, lam_k1, lam_q2, lam_k2,
              b_subln_g, w_gate, b_gate, w_br_a, w_br_b, w_out, norm2_g, w_fc1, w_fc2, final_norm_g):
    rope = axial_rope_tables(x_sample.shape[1], x_sample.dtype)
    xp, xs = x_prompt, x_sample
    new_a_k, new_a_v, new_b_k, new_b_v = [], [], [], []
    for l in range(DEPTH):
        lp = {
            'norm1_g': norm1_g[l], 'w_in': w_in[l], 'a_q_norm_g': a_q_norm_g[l], 'a_k_norm_g': a_k_norm_g[l],
            'lam_q1': lam_q1[l], 'lam_k1': lam_k1[l], 'lam_q2': lam_q2[l], 'lam_k2': lam_k2[l],
            'b_subln_g': b_subln_g[l], 'w_gate': w_gate[l], 'b_gate': b_gate[l],
            'w_br_a': w_br_a[l], 'w_br_b': w_br_b[l], 'w_out': w_out[l],
            'norm2_g': norm2_g[l], 'w_fc1': w_fc1[l], 'w_fc2': w_fc2[l],
        }
        lam_init = 0.8 - 0.6 * math.exp(-0.3 * l)
        mods_ctx = modulation(c_ctx, w_mod[l], b_mod[l])
        mods_lat = [m[:, None, :] for m in modulation(c, w_mod[l], b_mod[l])]
        xp, kv_ctx = trunk_layer(xp, mods_ctx, lp, lam_init)
        new_a_k.append(kv_ctx[0])
        new_a_v.append(kv_ctx[1])
        new_b_k.append(kv_ctx[2])
        new_b_v.append(kv_ctx[3])
        ctx_kv = (cache_a_k[:, l], cache_a_v[:, l], cache_b_k[:, l], cache_b_v[:, l])
        xs, _ = trunk_layer(xs, mods_lat, lp, lam_init, rope=rope, ctx_kv=ctx_kv)
    y_prompt = rms_norm(xp, final_norm_g)
    y_sample = rms_norm(xs, final_norm_g)
    return (y_prompt, y_sample, jnp.stack(new_a_k, axis=1), jnp.stack(new_a_v, axis=1),
            jnp.stack(new_b_k, axis=1), jnp.stack(new_b_v, axis=1))
```

```python
import functools

import jax
import jax.numpy as jnp
from jax import lax
from jax.experimental import pallas as pl
from jax.experimental.pallas import tpu as pltpu

F32 = jnp.float32
BF16 = jnp.bfloat16

D_MODEL = 1024
HEAD_DIM = 64
HALF = HEAD_DIM // 2
GRID_W = 64
A_Q_HEADS = 8
A_KV_HEADS = 2
A_GROUP = A_Q_HEADS // A_KV_HEADS
B_HEADS = 4
B_V_DIM = 2 * HEAD_DIM
A_Q_COLS = A_Q_HEADS * HEAD_DIM
A_KV_COLS = A_KV_HEADS * HEAD_DIM
B_QK_COLS = B_HEADS * 2 * HEAD_DIM
B_V_COLS = B_HEADS * B_V_DIM
OFF_AQ = 0
OFF_AK = OFF_AQ + A_Q_COLS
OFF_AV = OFF_AK + A_KV_COLS
OFF_BQ = OFF_AV + A_KV_COLS
OFF_BK = OFF_BQ + B_QK_COLS
OFF_BV = OFF_BK + B_QK_COLS
IN_COLS = OFF_BV + B_V_COLS
D_FF = 4 * D_MODEL
N_MOD = 6
ROPE_BASE = 10000.0
EPS = 1e-6
LAM_INIT = 0.2
QK_SCALE = HEAD_DIM ** -0.5
NEG_BIG = -1e30

V7X_VMEM_BYTES = 64 * 1024 * 1024
VMEM_LIMIT = 56 * 1024 * 1024


def _rms(x, axis):
    return x * lax.rsqrt(jnp.mean(x * x, axis=axis, keepdims=True) + EPS)


def _modulated_norm(x, g, scale, shift):
    return (_rms(x, -1) * g) * (1.0 + scale) + shift


def _mod_kernel(cond_ref, w_ref, b_ref, o_ref):
    c = cond_ref[...]
    s = c * jax.nn.sigmoid(c)
    o_ref[...] = jnp.dot(s.astype(BF16), w_ref[...].astype(BF16),
                         preferred_element_type=F32) + b_ref[...]


def _modulation(cond, w_mod, b_mod):
    rows = cond.shape[0]
    n = w_mod.shape[1]
    tn = 1536
    return pl.pallas_call(
        _mod_kernel,
        grid=(n // tn,),
        in_specs=[
            pl.BlockSpec((rows, D_MODEL), lambda j: (0, 0)),
            pl.BlockSpec((D_MODEL, tn), lambda j: (0, j)),
            pl.BlockSpec((1, tn), lambda j: (0, j)),
        ],
        out_specs=pl.BlockSpec((rows, tn), lambda j: (0, j)),
        out_shape=jax.ShapeDtypeStruct((rows, n), F32),
        compiler_params=pltpu.CompilerParams(
            dimension_semantics=("arbitrary",), vmem_limit_bytes=VMEM_LIMIT),
        name="modulation",
    )(cond, w_mod, b_mod.reshape(1, n))


def _rope_t(x, cos, sin):
    x1, x2 = x[:HALF], x[HALF:]
    return jnp.concatenate([x1 * cos - x2 * sin, x1 * sin + x2 * cos], axis=0)


def _proj_kernel(*refs, use_rope, emit_cache):
    it = iter(refs)
    x_ref, mods_ref, g1_ref, w_ref, gq_ref, gk_ref = (next(it) for _ in range(6))
    cos_ref = sin_ref = None
    if use_rope:
        cos_ref, sin_ref = next(it), next(it)
    qa_ref, ka_ref, va_ref, qb_ref, kb_ref, vb_ref = (next(it) for _ in range(6))
    if emit_cache:
        cak_ref, cav_ref, cbk_ref, cbv_ref = (next(it) for _ in range(4))

    x = x_ref[0]
    mods = mods_ref[0]
    h = _modulated_norm(x, g1_ref[...], mods[1:2], mods[0:1])
    t = lax.dot_general(w_ref[...], h.astype(BF16), (((1,), (1,)), ((), ())),
                        preferred_element_type=F32)
    if use_rope:
        cos, sin = cos_ref[...], sin_ref[...]

    def head(off, gain=None):
        blk = t[off:off + HEAD_DIM]
        if gain is not None:
            blk = _rms(blk, 0) * gain
        if use_rope:
            blk = _rope_t(blk, cos, sin)
        return blk

    gq, gk = gq_ref[...], gk_ref[...]
    for i in range(A_Q_HEADS):
        off = OFF_AQ + i * HEAD_DIM
        qa_ref[0, i * HEAD_DIM:(i + 1) * HEAD_DIM, :] = (head(off, gq) * QK_SCALE).astype(BF16)
    for i in range(2 * B_HEADS):
        off = OFF_BQ + i * HEAD_DIM
        qb_ref[0, i * HEAD_DIM:(i + 1) * HEAD_DIM, :] = (head(off) * QK_SCALE).astype(BF16)

    ak_t = jnp.concatenate([head(OFF_AK + i * HEAD_DIM, gk) for i in range(A_KV_HEADS)], axis=0)
    ak = ak_t.T
    ka_ref[0] = ak.astype(BF16)
    bk_t = jnp.concatenate([head(OFF_BK + i * HEAD_DIM) for i in range(2 * B_HEADS)], axis=0)
    bk = bk_t.T
    kb_ref[0] = bk.astype(BF16)
    av_t = t[OFF_AV:OFF_AV + A_KV_COLS]
    bv_t = t[OFF_BV:OFF_BV + B_V_COLS]
    va_ref[0] = av_t.astype(BF16)
    vb_ref[0] = bv_t.astype(BF16)
    if emit_cache:
        cak_ref[0] = ak
        cbk_ref[0] = bk
        cav_ref[0] = av_t.T
        cbv_ref[0] = bv_t.T


def _projection(x, mods, mod_row, g1, w_in_t, gq, gk, rope, emit_cache, tm):
    b, s, _ = x.shape
    use_rope = rope is not None
    full = lambda shape: pl.BlockSpec(shape, lambda bi, i: (0,) * len(shape))
    in_specs = [
        pl.BlockSpec((1, tm, D_MODEL), lambda bi, i: (bi, i, 0)),
        pl.BlockSpec((1, N_MOD, D_MODEL), lambda bi, i: (mod_row(bi), 0, 0)),
        full((1, D_MODEL)),
        full((IN_COLS, D_MODEL)),
        full((HEAD_DIM, 1)),
        full((HEAD_DIM, 1)),
    ]
    args = [x, mods, g1, w_in_t, gq, gk]
    if use_rope:
        in_specs += [pl.BlockSpec((HALF, tm), lambda bi, i: (0, i))] * 2
        args += list(rope)
    fm = lambda rows: pl.BlockSpec((1, rows, tm), lambda bi, i: (bi, 0, i))
    tk = lambda cols: pl.BlockSpec((1, tm, cols), lambda bi, i: (bi, i, 0))
    out_specs = [fm(A_Q_COLS), tk(A_KV_COLS), fm(A_KV_COLS), fm(B_QK_COLS), tk(B_QK_COLS), fm(B_V_COLS)]
    out_shape = [
        jax.ShapeDtypeStruct((b, A_Q_COLS, s), BF16),
        jax.ShapeDtypeStruct((b, s, A_KV_COLS), BF16),
        jax.ShapeDtypeStruct((b, A_KV_COLS, s), BF16),
        jax.ShapeDtypeStruct((b, B_QK_COLS, s), BF16),
        jax.ShapeDtypeStruct((b, s, B_QK_COLS), BF16),
        jax.ShapeDtypeStruct((b, B_V_COLS, s), BF16),
    ]
    if emit_cache:
        out_specs += [tk(A_KV_COLS), tk(A_KV_COLS), tk(B_QK_COLS), tk(B_V_COLS)]
        out_shape += [
            jax.ShapeDtypeStruct((b, s, A_KV_COLS), F32),
            jax.ShapeDtypeStruct((b, s, A_KV_COLS), F32),
            jax.ShapeDtypeStruct((b, s, B_QK_COLS), F32),
            jax.ShapeDtypeStruct((b, s, B_V_COLS), F32),
        ]
    return pl.pallas_call(
        functools.partial(_proj_kernel, use_rope=use_rope, emit_cache=emit_cache),
        grid=(b, s // tm),
        in_specs=in_specs,
        out_specs=out_specs,
        out_shape=out_shape,
        compiler_params=pltpu.CompilerParams(
            dimension_semantics=("arbitrary", "arbitrary"), vmem_limit_bytes=VMEM_LIMIT),
        name="projection_rope" if use_rope else "projection_ctx",
    )(*args)


def _attn_kernel(qa_ref, qb_ref, ka_ref, va_ref, kb_ref, vb_ref, lamv_ref, gs_ref,
                 oa_ref, ob_ref, qpad_sc, m_sc, l_sc, acc_sc, *, tq, tk, n_kv):
    def flash(n, k_ref, k_lanes, v_ref, v_rows, dv):
        m_sc[:, :n] = jnp.full((1, n), NEG_BIG, F32)
        l_sc[:, :n] = jnp.zeros((1, n), F32)
        acc_sc[:dv, :n] = jnp.zeros((dv, n), F32)

        def body(j, carry):
            off = pl.multiple_of(j * tk, tk)
            kb = k_ref[0, pl.ds(off, tk), k_lanes]
            s = jnp.dot(kb, qpad_sc[:, :n], preferred_element_type=F32)
            m_prev = m_sc[:, :n]
            m_new = jnp.maximum(m_prev, jnp.max(s, axis=0, keepdims=True))
            alpha = jnp.exp(m_prev - m_new)
            p = jnp.exp(s - m_new)
            l_sc[:, :n] = alpha * l_sc[:, :n] + jnp.sum(p, axis=0, keepdims=True)
            vb = v_ref[0, v_rows, pl.ds(off, tk)]
            acc_sc[:dv, :n] = alpha * acc_sc[:dv, :n] + jnp.dot(
                vb, p.astype(BF16), preferred_element_type=F32)
            m_sc[:, :n] = m_new
            return carry

        lax.fori_loop(0, n_kv, body, 0)
        return acc_sc[:dv, :n] * (1.0 / l_sc[:, :n])

    zeros = jnp.zeros((HEAD_DIM, tq), BF16)

    n_a = A_GROUP * tq
    for g in range(A_KV_HEADS):
        for r in range(A_GROUP):
            hd = g * A_GROUP + r
            q = qa_ref[0, hd * HEAD_DIM:(hd + 1) * HEAD_DIM, :]
            pieces = [zeros, zeros]
            pieces[g] = q
            qpad_sc[:, r * tq:(r + 1) * tq] = jnp.concatenate(pieces, axis=0)
        o = flash(n_a, ka_ref, slice(None), va_ref, slice(g * HEAD_DIM, (g + 1) * HEAD_DIM), HEAD_DIM)
        for r in range(A_GROUP):
            hd = g * A_GROUP + r
            oa_ref[0, hd * HEAD_DIM:(hd + 1) * HEAD_DIM, :] = o[:, r * tq:(r + 1) * tq].astype(BF16)

    lv = lamv_ref[...]
    lam = (jnp.exp(jnp.sum(lv[0:1] * lv[1:2], axis=-1, keepdims=True))
           - jnp.exp(jnp.sum(lv[2:3] * lv[3:4], axis=-1, keepdims=True)) + LAM_INIT)
    n_b = 2 * tq
    for hd in range(B_HEADS):
        base = hd * B_V_DIM
        q1 = qb_ref[0, base:base + HEAD_DIM, :]
        q2 = qb_ref[0, base + HEAD_DIM:base + 2 * HEAD_DIM, :]
        qpad_sc[:, 0:tq] = jnp.concatenate([q1, zeros], axis=0)
        qpad_sc[:, tq:2 * tq] = jnp.concatenate([zeros, q2], axis=0)
        o = flash(n_b, kb_ref, slice(base, base + B_V_DIM), vb_ref, slice(base, base + B_V_DIM), B_V_DIM)
        d = o[:, :tq] - lam * o[:, tq:]
        d = _rms(d, 0) * gs_ref[...] * (1.0 - LAM_INIT)
        ob_ref[0, base:base + B_V_DIM, :] = d.astype(BF16)


def _attention(qa, qb, ka, va, kb, vb, lamv, gs, tq, tk):
    b, _, s = qa.shape
    t = ka.shape[1]
    n_max = A_GROUP * tq
    qspec = pl.BlockSpec((1, A_Q_COLS, tq), lambda bi, i: (bi, 0, i))
    whole = lambda a: pl.BlockSpec((1,) + a.shape[1:], lambda bi, i: (bi, 0, 0))
    full = lambda a: pl.BlockSpec(a.shape, lambda bi, i: (0,) * a.ndim)
    return pl.pallas_call(
        functools.partial(_attn_kernel, tq=tq, tk=tk, n_kv=t // tk),
        grid=(b, s // tq),
        in_specs=[qspec, qspec, whole(ka), whole(va), whole(kb), whole(vb), full(lamv), full(gs)],
        out_specs=[qspec, qspec],
        out_shape=[jax.ShapeDtypeStruct((b, A_Q_COLS, s), BF16),
                   jax.ShapeDtypeStruct((b, B_V_COLS, s), BF16)],
        scratch_shapes=[
            pltpu.VMEM((2 * HEAD_DIM, n_max), BF16),
            pltpu.VMEM((1, n_max), F32),
            pltpu.VMEM((1, n_max), F32),
            pltpu.VMEM((B_V_DIM, n_max), F32),
        ],
        compiler_params=pltpu.CompilerParams(
            dimension_semantics=("arbitrary", "arbitrary"), vmem_limit_bytes=VMEM_LIMIT),
        name="attention_t%d" % t,
    )(qa, qb, ka, va, kb, vb, lamv, gs)


def _out_kernel(x_ref, mods_ref, g1_ref, g2_ref, gf_ref, oa_ref, ob_ref, wg_ref, bg_ref,
                wa_ref, wb_ref, wo_ref, w1_ref, w2_ref, y_ref):
    x = x_ref[0]
    mods = mods_ref[0]
    h = _modulated_norm(x, g1_ref[...], mods[1:2], mods[0:1])
    gates = jax.nn.sigmoid(
        jnp.dot(h.astype(BF16), wg_ref[...], preferred_element_type=F32) + bg_ref[...])
    tn = (((0,), (0,)), ((), ()))
    br_a = lax.dot_general(oa_ref[0], wa_ref[...], tn, preferred_element_type=F32)
    br_b = lax.dot_general(ob_ref[0], wb_ref[...], tn, preferred_element_type=F32)
    merged = gates[:, :D_MODEL] * br_a + gates[:, D_MODEL:] * br_b
    x = x + mods[2:3] * jnp.dot(merged.astype(BF16), wo_ref[...], preferred_element_type=F32)
    h2 = _modulated_norm(x, g2_ref[...], mods[4:5], mods[3:4])
    f = jnp.dot(h2.astype(BF16), w1_ref[...], preferred_element_type=F32)
    f = jnp.square(jnp.maximum(f, 0.0))
    x = x + mods[5:6] * jnp.dot(f.astype(BF16), w2_ref[...], preferred_element_type=F32)
    y_ref[0] = _rms(x, -1) * gf_ref[...]


def _output(x, mods, mod_row, g1, g2, gf, oa, ob, wg, bg, wa, wb, wo, w1, w2, tm):
    b, s, _ = x.shape
    full = lambda a: pl.BlockSpec(a.shape, lambda bi, i: (0,) * a.ndim)
    xspec = pl.BlockSpec((1, tm, D_MODEL), lambda bi, i: (bi, i, 0))
    ospec = pl.BlockSpec((1, A_Q_COLS, tm), lambda bi, i: (bi, 0, i))
    return pl.pallas_call(
        _out_kernel,
        grid=(b, s // tm),
        in_specs=[xspec,
                  pl.BlockSpec((1, N_MOD, D_MODEL), lambda bi, i: (mod_row(bi), 0, 0)),
                  full(g1), full(g2), full(gf), ospec, ospec,
                  full(wg), full(bg), full(wa), full(wb), full(wo), full(w1), full(w2)],
        out_specs=xspec,
        out_shape=jax.ShapeDtypeStruct(x.shape, F32),
        compiler_params=pltpu.CompilerParams(
            dimension_semantics=("arbitrary", "arbitrary"), vmem_limit_bytes=VMEM_LIMIT),
        name="output_s%d" % s,
    )(x, mods, g1, g2, gf, oa, ob, wg, bg, wa, wb, wo, w1, w2)


def _rope_tables_t(n_tokens):
    pos = jnp.arange(n_tokens, dtype=jnp.int32)
    row = (pos // GRID_W).astype(F32)
    col = (pos % GRID_W).astype(F32)
    n_freq = HEAD_DIM // 4
    freqs = ROPE_BASE ** (-jnp.arange(n_freq, dtype=F32) / n_freq)
    ang = jnp.concatenate([freqs[:, None] * row[None, :], freqs[:, None] * col[None, :]], axis=0)
    return jnp.cos(ang), jnp.sin(ang)


def kernel(x_prompt, x_sample, cache_a_k, cache_a_v, cache_b_k, cache_b_v, c, c_ctx, w_mod, b_mod, norm1_g, w_in, a_q_norm_g, a_k_norm_g, lam_q1, lam_k1, lam_q2, lam_k2, b_subln_g, w_gate, b_gate, w_br_a, w_br_b, w_out, norm2_g, w_fc1, w_fc2, final_norm_g):
    nb, seq, _ = x_prompt.shape
    db, dseq, _ = x_sample.shape
    past = cache_a_k.shape[2]
    l = 0

    cond = jnp.concatenate([c, c_ctx[None, :]], axis=0)
    ctx_row = db
    rows = -(-cond.shape[0] // 8) * 8
    cond = jnp.pad(cond, ((0, rows - cond.shape[0]), (0, 0)))
    mods = _modulation(cond, w_mod[l], b_mod[l]).reshape(rows, N_MOD, D_MODEL)

    g1 = norm1_g[l].reshape(1, D_MODEL)
    g2 = norm2_g[l].reshape(1, D_MODEL)
    gf = final_norm_g.reshape(1, D_MODEL)
    gq = a_q_norm_g[l].reshape(HEAD_DIM, 1)
    gk = a_k_norm_g[l].reshape(HEAD_DIM, 1)
    gs = b_subln_g[l].reshape(B_V_DIM, 1)
    lamv = jnp.stack([lam_q1[l], lam_k1[l], lam_q2[l], lam_k2[l]], axis=0)
    w_in_t = w_in[l].T.astype(BF16)
    wg = w_gate[l].astype(BF16)
    bg = b_gate[l].reshape(1, -1)
    wa = w_br_a[l].astype(BF16)
    wb = w_br_b[l].astype(BF16)
    wo = w_out[l].astype(BF16)
    w1 = w_fc1[l].astype(BF16)
    w2 = w_fc2[l].astype(BF16)

    ctx_mod = lambda bi: ctx_row
    lat_mod = lambda bi: bi

    tm_c = min(seq, 512)
    qa, ka, va, qb, kb, vb, new_ak, new_av, new_bk, new_bv = _projection(
        x_prompt, mods, ctx_mod, g1, w_in_t, gq, gk, None, True, tm_c)
    oa, ob = _attention(qa, qb, ka, va, kb, vb, lamv, gs, tq=min(seq, 256), tk=min(seq, 512))
    y_prompt = _output(x_prompt, mods, ctx_mod, g1, g2, gf, oa, ob, wg, bg, wa, wb, wo, w1, w2, tm_c)

    rope = _rope_tables_t(dseq)
    qa, ka, va, qb, kb, vb = _projection(
        x_sample, mods, lat_mod, g1, w_in_t, gq, gk, rope, False, 512)
    ka = jnp.concatenate([cache_a_k[:, l].reshape(db, past, A_KV_COLS).astype(BF16), ka], axis=1)
    kb = jnp.concatenate([cache_b_k[:, l].reshape(db, past, B_QK_COLS).astype(BF16), kb], axis=1)
    va = jnp.concatenate(
        [jnp.swapaxes(cache_a_v[:, l].reshape(db, past, A_KV_COLS), 1, 2).astype(BF16), va], axis=2)
    vb = jnp.concatenate(
        [jnp.swapaxes(cache_b_v[:, l].reshape(db, past, B_V_COLS), 1, 2).astype(BF16), vb], axis=2)
    oa, ob = _attention(qa, qb, ka, va, kb, vb, lamv, gs, tq=256, tk=512)
    y_sample = _output(x_sample, mods, lat_mod, g1, g2, gf, oa, ob, wg, bg, wa, wb, wo, w1, w2, 512)

    return (y_prompt, y_sample,
            new_ak.reshape(nb, 1, seq, A_KV_HEADS, HEAD_DIM),
            new_av.reshape(nb, 1, seq, A_KV_HEADS, HEAD_DIM),
            new_bk.reshape(nb, 1, seq, B_HEADS, 2, HEAD_DIM),
            new_bv.reshape(nb, 1, seq, B_HEADS, B_V_DIM))
```

```python
import functools

import jax
import jax.numpy as jnp
from jax import lax
from jax.experimental import pallas as pl
from jax.experimental.pallas import tpu as pltpu

F32 = jnp.float32
BF16 = jnp.bfloat16

D_MODEL = 1024
HEAD_DIM = 64
HALF = HEAD_DIM // 2
GRID_W = 64
A_Q_HEADS = 8
A_KV_HEADS = 2
A_GROUP = A_Q_HEADS // A_KV_HEADS
B_HEADS = 4
B_V_DIM = 2 * HEAD_DIM
A_Q_COLS = A_Q_HEADS * HEAD_DIM
A_KV_COLS = A_KV_HEADS * HEAD_DIM
B_QK_COLS = B_HEADS * 2 * HEAD_DIM
B_V_COLS = B_HEADS * B_V_DIM
OFF_AQ = 0
OFF_AK = OFF_AQ + A_Q_COLS
OFF_AV = OFF_AK + A_KV_COLS
OFF_BQ = OFF_AV + A_KV_COLS
OFF_BK = OFF_BQ + B_QK_COLS
OFF_BV = OFF_BK + B_QK_COLS
IN_COLS = OFF_BV + B_V_COLS
D_FF = 4 * D_MODEL
N_MOD = 6
ROPE_BASE = 10000.0
EPS = 1e-6
LAM_INIT = 0.2
LOG2E = 1.4426950408889634
Q_PRESCALE = HEAD_DIM ** -0.5 * LOG2E
NEG_BIG = -1e30

V7X_VMEM_BYTES = 64 * 1024 * 1024
VMEM_LIMIT = 56 * 1024 * 1024


def _rms(x, axis):
    return x * lax.rsqrt(jnp.mean(x * x, axis=axis, keepdims=True) + EPS)


def _modulated_norm(x, g, scale, shift):
    return (_rms(x, -1) * g) * (1.0 + scale) + shift


def _mod_kernel(cond_ref, w_ref, b_ref, o_ref):
    c = cond_ref[...]
    s = c * jax.nn.sigmoid(c)
    o_ref[...] = jnp.dot(s.astype(BF16), w_ref[...].astype(BF16),
                         preferred_element_type=F32) + b_ref[...]


def _modulation(cond, w_mod, b_mod):
    rows = cond.shape[0]
    n = w_mod.shape[1]
    tn = 1536
    return pl.pallas_call(
        _mod_kernel,
        grid=(n // tn,),
        in_specs=[
            pl.BlockSpec((rows, D_MODEL), lambda j: (0, 0)),
            pl.BlockSpec((D_MODEL, tn), lambda j: (0, j)),
            pl.BlockSpec((1, tn), lambda j: (0, j)),
        ],
        out_specs=pl.BlockSpec((rows, tn), lambda j: (0, j)),
        out_shape=jax.ShapeDtypeStruct((rows, n), F32),
        compiler_params=pltpu.CompilerParams(
            dimension_semantics=("arbitrary",), vmem_limit_bytes=VMEM_LIMIT),
        name="modulation",
    )(cond, w_mod, b_mod.reshape(1, n))


def _rope_t(x, cos, sin):
    x1, x2 = x[:HALF], x[HALF:]
    return jnp.concatenate([x1 * cos - x2 * sin, x1 * sin + x2 * cos], axis=0)


def _proj_kernel(*refs, use_rope, emit_cache):
    it = iter(refs)
    x_ref, mods_ref, g1_ref, w_ref, gq_ref, gk_ref = (next(it) for _ in range(6))
    cos_ref = sin_ref = None
    if use_rope:
        cos_ref, sin_ref = next(it), next(it)
    qa_ref, ka_ref, va_ref, qb_ref, kb_ref, vb_ref = (next(it) for _ in range(6))
    if emit_cache:
        cak_ref, cav_ref, cbk_ref, cbv_ref = (next(it) for _ in range(4))

    x = x_ref[0]
    mods = mods_ref[0]
    h = _modulated_norm(x, g1_ref[...], mods[1:2], mods[0:1])
    t = lax.dot_general(w_ref[...], h.astype(BF16), (((1,), (1,)), ((), ())),
                        preferred_element_type=F32)
    if use_rope:
        cos, sin = cos_ref[...], sin_ref[...]

    def head(off, gain=None):
        blk = t[off:off + HEAD_DIM]
        if gain is not None:
            blk = _rms(blk, 0) * gain
        if use_rope:
            blk = _rope_t(blk, cos, sin)
        return blk

    gq, gk = gq_ref[...], gk_ref[...]
    for i in range(A_Q_HEADS):
        off = OFF_AQ + i * HEAD_DIM
        qa_ref[0, i * HEAD_DIM:(i + 1) * HEAD_DIM, :] = (head(off, gq) * Q_PRESCALE).astype(BF16)
    for i in range(2 * B_HEADS):
        off = OFF_BQ + i * HEAD_DIM
        qb_ref[0, i * HEAD_DIM:(i + 1) * HEAD_DIM, :] = (head(off) * Q_PRESCALE).astype(BF16)

    ak_t = jnp.concatenate([head(OFF_AK + i * HEAD_DIM, gk) for i in range(A_KV_HEADS)], axis=0)
    ak = ak_t.T
    ka_ref[0] = ak.astype(BF16)
    bk_t = jnp.concatenate([head(OFF_BK + i * HEAD_DIM) for i in range(2 * B_HEADS)], axis=0)
    bk = bk_t.T
    kb_ref[0] = bk.astype(BF16)
    av_t = t[OFF_AV:OFF_AV + A_KV_COLS]
    bv_t = t[OFF_BV:OFF_BV + B_V_COLS]
    va_ref[0] = av_t.astype(BF16)
    vb_ref[0] = bv_t.astype(BF16)
    if emit_cache:
        cak_ref[0] = ak
        cbk_ref[0] = bk
        cav_ref[0] = av_t.T
        cbv_ref[0] = bv_t.T


def _projection(x, mods, mod_row, g1, w_in_t, gq, gk, rope, emit_cache, tm):
    b, s, _ = x.shape
    use_rope = rope is not None
    full = lambda shape: pl.BlockSpec(shape, lambda bi, i: (0,) * len(shape))
    in_specs = [
        pl.BlockSpec((1, tm, D_MODEL), lambda bi, i: (bi, i, 0)),
        pl.BlockSpec((1, N_MOD, D_MODEL), lambda bi, i: (mod_row(bi), 0, 0)),
        full((1, D_MODEL)),
        full((IN_COLS, D_MODEL)),
        full((HEAD_DIM, 1)),
        full((HEAD_DIM, 1)),
    ]
    args = [x, mods, g1, w_in_t, gq, gk]
    if use_rope:
        in_specs += [pl.BlockSpec((HALF, tm), lambda bi, i: (0, i))] * 2
        args += list(rope)
    fm = lambda rows: pl.BlockSpec((1, rows, tm), lambda bi, i: (bi, 0, i))
    tk = lambda cols: pl.BlockSpec((1, tm, cols), lambda bi, i: (bi, i, 0))
    out_specs = [fm(A_Q_COLS), tk(A_KV_COLS), fm(A_KV_COLS), fm(B_QK_COLS), tk(B_QK_COLS), fm(B_V_COLS)]
    out_shape = [
        jax.ShapeDtypeStruct((b, A_Q_COLS, s), BF16),
        jax.ShapeDtypeStruct((b, s, A_KV_COLS), BF16),
        jax.ShapeDtypeStruct((b, A_KV_COLS, s), BF16),
        jax.ShapeDtypeStruct((b, B_QK_COLS, s), BF16),
        jax.ShapeDtypeStruct((b, s, B_QK_COLS), BF16),
        jax.ShapeDtypeStruct((b, B_V_COLS, s), BF16),
    ]
    if emit_cache:
        out_specs += [tk(A_KV_COLS), tk(A_KV_COLS), tk(B_QK_COLS), tk(B_V_COLS)]
        out_shape += [
            jax.ShapeDtypeStruct((b, s, A_KV_COLS), F32),
            jax.ShapeDtypeStruct((b, s, A_KV_COLS), F32),
            jax.ShapeDtypeStruct((b, s, B_QK_COLS), F32),
            jax.ShapeDtypeStruct((b, s, B_V_COLS), F32),
        ]
    return pl.pallas_call(
        functools.partial(_proj_kernel, use_rope=use_rope, emit_cache=emit_cache),
        grid=(b, s // tm),
        in_specs=in_specs,
        out_specs=out_specs,
        out_shape=out_shape,
        compiler_params=pltpu.CompilerParams(
            dimension_semantics=("arbitrary", "arbitrary"), vmem_limit_bytes=VMEM_LIMIT),
        name="projection_rope" if use_rope else "projection_ctx",
    )(*args)


ONES_ROWS = 16
ATTN_LOOPS = (("a0", "a1"), ("b0", "b1", "b2", "b3"))


def _attn_kernel(qa_ref, qb_ref, ka_ref, va_ref, kb_ref, vb_ref, lamv_ref, gs_ref,
                 oa_ref, ob_ref, qpad_sc, m_sc, acc_sc, s_sc, bm_sc, *, tq, tk, n_kv, loops):
    zeros = jnp.zeros((HEAD_DIM, tq), BF16)
    ones = jnp.ones((ONES_ROWS, tk), BF16)

    streams = {}
    lane = 0
    n_a = A_GROUP * tq
    for g in range(A_KV_HEADS):
        for r in range(A_GROUP):
            hd = g * A_GROUP + r
            q = qa_ref[0, hd * HEAD_DIM:(hd + 1) * HEAD_DIM, :]
            pieces = [zeros, zeros]
            pieces[g] = q
            qpad_sc[:, lane + r * tq:lane + (r + 1) * tq] = jnp.concatenate(pieces, axis=0)
        streams["a%d" % g] = (lane, n_a, ka_ref, slice(None), va_ref,
                              slice(g * HEAD_DIM, (g + 1) * HEAD_DIM), HEAD_DIM)
        lane += n_a
    n_b = 2 * tq
    for hd in range(B_HEADS):
        base = hd * B_V_DIM
        q1 = qb_ref[0, base:base + HEAD_DIM, :]
        q2 = qb_ref[0, base + HEAD_DIM:base + 2 * HEAD_DIM, :]
        qpad_sc[:, lane:lane + tq] = jnp.concatenate([q1, zeros], axis=0)
        qpad_sc[:, lane + tq:lane + 2 * tq] = jnp.concatenate([zeros, q2], axis=0)
        streams["b%d" % hd] = (lane, n_b, kb_ref, slice(base, base + B_V_DIM), vb_ref,
                               slice(base, base + B_V_DIM), B_V_DIM)
        lane += n_b

    m_sc[...] = jnp.full(m_sc.shape, NEG_BIG, F32)
    acc_sc[...] = jnp.zeros(acc_sc.shape, F32)

    def block_offset(j):
        if isinstance(j, int):
            return j * tk
        return pl.multiple_of(j * tk, tk)

    def scores(j, slot, unit):
        name, c = unit
        lo, n, k_ref, k_lanes, v_ref, v_rows, dv = streams[name]
        lo = lo + c * tq
        off = block_offset(j)
        kb = k_ref[0, pl.ds(off, tk), k_lanes]
        s = jnp.dot(kb, qpad_sc[:, lo:lo + tq], preferred_element_type=F32)
        s_sc[slot, :, lo:lo + tq] = s
        bm_sc[slot, :, lo:lo + tq] = jnp.max(s, axis=0, keepdims=True)

    def finish(j, slot, unit):
        name, c = unit
        lo, n, k_ref, k_lanes, v_ref, v_rows, dv = streams[name]
        lo = lo + c * tq
        off = block_offset(j)
        m_prev = m_sc[:, lo:lo + tq]
        m_new = jnp.maximum(m_prev, bm_sc[slot, :, lo:lo + tq])
        alpha = jnp.exp2(m_prev - m_new)
        p = jnp.exp2(s_sc[slot, :, lo:lo + tq] - m_new).astype(BF16)
        vext = jnp.concatenate([v_ref[0, v_rows, pl.ds(off, tk)], ones], axis=0)
        rows = dv + ONES_ROWS
        acc_sc[:rows, lo:lo + tq] = alpha * acc_sc[:rows, lo:lo + tq] + jnp.dot(
            vext, p, preferred_element_type=F32)
        m_sc[:, lo:lo + tq] = m_new

    for names in loops:
        units = [(name, c) for name in names for c in range(streams[name][1] // tq)]

        def advance(j, slot, units=units):
            for unit in units:
                scores(j + 1, 1 - slot, unit)
                finish(j, slot, unit)

        def body(i, carry, advance=advance):
            advance(2 * i, 0)
            advance(2 * i + 1, 1)
            return carry

        for unit in units:
            scores(0, 0, unit)
        n_pairs = (n_kv - 1) // 2
        if n_pairs > 0:
            lax.fori_loop(0, n_pairs, body, 0)
        for j in range(2 * n_pairs, n_kv - 1):
            advance(j, j % 2)
        for unit in units:
            finish(n_kv - 1, (n_kv - 1) % 2, unit)

    def result(name):
        lo, n, _, _, _, _, dv = streams[name]
        return acc_sc[:dv, lo:lo + n] * (1.0 / acc_sc[dv:dv + 1, lo:lo + n])

    for g in range(A_KV_HEADS):
        o = result("a%d" % g)
        for r in range(A_GROUP):
            hd = g * A_GROUP + r
            oa_ref[0, hd * HEAD_DIM:(hd + 1) * HEAD_DIM, :] = o[:, r * tq:(r + 1) * tq].astype(BF16)

    lv = lamv_ref[...]
    lam = (jnp.exp(jnp.sum(lv[0:1] * lv[1:2], axis=-1, keepdims=True))
           - jnp.exp(jnp.sum(lv[2:3] * lv[3:4], axis=-1, keepdims=True)) + LAM_INIT)
    for hd in range(B_HEADS):
        base = hd * B_V_DIM
        o = result("b%d" % hd)
        d = o[:, :tq] - lam * o[:, tq:]
        d = _rms(d, 0) * gs_ref[...] * (1.0 - LAM_INIT)
        ob_ref[0, base:base + B_V_DIM, :] = d.astype(BF16)


def _attention(qa, qb, ka, va, kb, vb, lamv, gs, tq, tk):
    b, _, s = qa.shape
    t = ka.shape[1]
    lanes = A_KV_HEADS * A_GROUP * tq + B_HEADS * 2 * tq
    qspec = pl.BlockSpec((1, A_Q_COLS, tq), lambda bi, i: (bi, 0, i))
    whole = lambda a: pl.BlockSpec((1,) + a.shape[1:], lambda bi, i: (bi, 0, 0))
    full = lambda a: pl.BlockSpec(a.shape, lambda bi, i: (0,) * a.ndim)
    return pl.pallas_call(
        functools.partial(_attn_kernel, tq=tq, tk=tk, n_kv=t // tk, loops=ATTN_LOOPS),
        grid=(b, s // tq),
        in_specs=[qspec, qspec, whole(ka), whole(va), whole(kb), whole(vb), full(lamv), full(gs)],
        out_specs=[qspec, qspec],
        out_shape=[jax.ShapeDtypeStruct((b, A_Q_COLS, s), BF16),
                   jax.ShapeDtypeStruct((b, B_V_COLS, s), BF16)],
        scratch_shapes=[
            pltpu.VMEM((2 * HEAD_DIM, lanes), BF16),
            pltpu.VMEM((1, lanes), F32),
            pltpu.VMEM((B_V_DIM + ONES_ROWS, lanes), F32),
            pltpu.VMEM((2, tk, lanes), F32),
            pltpu.VMEM((2, 1, lanes), F32),
        ],
        compiler_params=pltpu.CompilerParams(
            dimension_semantics=("arbitrary", "arbitrary"), vmem_limit_bytes=VMEM_LIMIT),
        name="attention_t%d" % t,
    )(qa, qb, ka, va, kb, vb, lamv, gs)


def _out_kernel(x_ref, mods_ref, g1_ref, g2_ref, gf_ref, oa_ref, ob_ref, wg_ref, bg_ref,
                wa_ref, wb_ref, wo_ref, w1_ref, w2_ref, y_ref):
    x = x_ref[0]
    mods = mods_ref[0]
    h = _modulated_norm(x, g1_ref[...], mods[1:2], mods[0:1])
    gates = jax.nn.sigmoid(
        jnp.dot(h.astype(BF16), wg_ref[...], preferred_element_type=F32) + bg_ref[...])
    tn = (((0,), (0,)), ((), ()))
    br_a = lax.dot_general(oa_ref[0], wa_ref[...], tn, preferred_element_type=F32)
    br_b = lax.dot_general(ob_ref[0], wb_ref[...], tn, preferred_element_type=F32)
    merged = gates[:, :D_MODEL] * br_a + gates[:, D_MODEL:] * br_b
    x = x + mods[2:3] * jnp.dot(merged.astype(BF16), wo_ref[...], preferred_element_type=F32)
    h2 = _modulated_norm(x, g2_ref[...], mods[4:5], mods[3:4])
    f = jnp.dot(h2.astype(BF16), w1_ref[...], preferred_element_type=F32)
    f = jnp.square(jnp.maximum(f, 0.0))
    x = x + mods[5:6] * jnp.dot(f.astype(BF16), w2_ref[...], preferred_element_type=F32)
    y_ref[0] = _rms(x, -1) * gf_ref[...]


def _output(x, mods, mod_row, g1, g2, gf, oa, ob, wg, bg, wa, wb, wo, w1, w2, tm):
    b, s, _ = x.shape
    full = lambda a: pl.BlockSpec(a.shape, lambda bi, i: (0,) * a.ndim)
    xspec = pl.BlockSpec((1, tm, D_MODEL), lambda bi, i: (bi, i, 0))
    ospec = pl.BlockSpec((1, A_Q_COLS, tm), lambda bi, i: (bi, 0, i))
    return pl.pallas_call(
        _out_kernel,
        grid=(b, s // tm),
        in_specs=[xspec,
                  pl.BlockSpec((1, N_MOD, D_MODEL), lambda bi, i: (mod_row(bi), 0, 0)),
                  full(g1), full(g2), full(gf), ospec, ospec,
                  full(wg), full(bg), full(wa), full(wb), full(wo), full(w1), full(w2)],
        out_specs=xspec,
        out_shape=jax.ShapeDtypeStruct(x.shape, F32),
        compiler_params=pltpu.CompilerParams(
            dimension_semantics=("arbitrary", "arbitrary"), vmem_limit_bytes=VMEM_LIMIT),
        name="output_s%d" % s,
    )(x, mods, g1, g2, gf, oa, ob, wg, bg, wa, wb, wo, w1, w2)


def _rope_tables_t(n_tokens):
    pos = jnp.arange(n_tokens, dtype=jnp.int32)
    row = (pos // GRID_W).astype(F32)
    col = (pos % GRID_W).astype(F32)
    n_freq = HEAD_DIM // 4
    freqs = ROPE_BASE ** (-jnp.arange(n_freq, dtype=F32) / n_freq)
    ang = jnp.concatenate([freqs[:, None] * row[None, :], freqs[:, None] * col[None, :]], axis=0)
    return jnp.cos(ang), jnp.sin(ang)


def kernel(x_prompt, x_sample, cache_a_k, cache_a_v, cache_b_k, cache_b_v, c, c_ctx, w_mod, b_mod, norm1_g, w_in, a_q_norm_g, a_k_norm_g, lam_q1, lam_k1, lam_q2, lam_k2, b_subln_g, w_gate, b_gate, w_br_a, w_br_b, w_out, norm2_g, w_fc1, w_fc2, final_norm_g):
    nb, seq, _ = x_prompt.shape
    db, dseq, _ = x_sample.shape
    past = cache_a_k.shape[2]
    l = 0

    cond = jnp.concatenate([c, c_ctx[None, :]], axis=0)
    ctx_row = db
    rows = -(-cond.shape[0] // 8) * 8
    cond = jnp.pad(cond, ((0, rows - cond.shape[0]), (0, 0)))
    mods = _modulation(cond, w_mod[l], b_mod[l]).reshape(rows, N_MOD, D_MODEL)

    g1 = norm1_g[l].reshape(1, D_MODEL)
    g2 = norm2_g[l].reshape(1, D_MODEL)
    gf = final_norm_g.reshape(1, D_MODEL)
    gq = a_q_norm_g[l].reshape(HEAD_DIM, 1)
    gk = a_k_norm_g[l].reshape(HEAD_DIM, 1)
    gs = b_subln_g[l].reshape(B_V_DIM, 1)
    lamv = jnp.stack([lam_q1[l], lam_k1[l], lam_q2[l], lam_k2[l]], axis=0)
    w_in_t = w_in[l].T.astype(BF16)
    wg = w_gate[l].astype(BF16)
    bg = b_gate[l].reshape(1, -1)
    wa = w_br_a[l].astype(BF16)
    wb = w_br_b[l].astype(BF16)
    wo = w_out[l].astype(BF16)
    w1 = w_fc1[l].astype(BF16)
    w2 = w_fc2[l].astype(BF16)

    ctx_mod = lambda bi: ctx_row
    lat_mod = lambda bi: bi

    tm_c = min(seq, 512)
    qa, ka, va, qb, kb, vb, new_ak, new_av, new_bk, new_bv = _projection(
        x_prompt, mods, ctx_mod, g1, w_in_t, gq, gk, None, True, tm_c)
    oa, ob = _attention(qa, qb, ka, va, kb, vb, lamv, gs, tq=min(seq, 256), tk=min(seq, 512))
    y_prompt = _output(x_prompt, mods, ctx_mod, g1, g2, gf, oa, ob, wg, bg, wa, wb, wo, w1, w2, tm_c)

    rope = _rope_tables_t(dseq)
    qa, ka, va, qb, kb, vb = _projection(
        x_sample, mods, lat_mod, g1, w_in_t, gq, gk, rope, False, 512)
    ka = jnp.concatenate([cache_a_k[:, l].reshape(db, past, A_KV_COLS).astype(BF16), ka], axis=1)
    kb = jnp.concatenate([cache_b_k[:, l].reshape(db, past, B_QK_COLS).astype(BF16), kb], axis=1)
    va = jnp.concatenate(
        [jnp.swapaxes(cache_a_v[:, l].reshape(db, past, A_KV_COLS), 1, 2).astype(BF16), va], axis=2)
    vb = jnp.concatenate(
        [jnp.swapaxes(cache_b_v[:, l].reshape(db, past, B_V_COLS), 1, 2).astype(BF16), vb], axis=2)
    oa, ob = _attention(qa, qb, ka, va, kb, vb, lamv, gs, tq=256, tk=512)
    y_sample = _output(x_sample, mods, lat_mod, g1, g2, gf, oa, ob, wg, bg, wa, wb, wo, w1, w2, 512)

    return (y_prompt, y_sample,
            new_ak.reshape(nb, 1, seq, A_KV_HEADS, HEAD_DIM),
            new_av.reshape(nb, 1, seq, A_KV_HEADS, HEAD_DIM),
            new_bk.reshape(nb, 1, seq, B_HEADS, 2, HEAD_DIM),
            new_bv.reshape(nb, 1, seq, B_HEADS, B_V_DIM))
```

```python
import functools

import jax
import jax.numpy as jnp
from jax import lax
from jax.experimental import pallas as pl
from jax.experimental.pallas import tpu as pltpu

F32 = jnp.float32
BF16 = jnp.bfloat16

D_MODEL = 1024
HEAD_DIM = 64
HALF = HEAD_DIM // 2
GRID_W = 64
A_Q_HEADS = 8
A_KV_HEADS = 2
A_GROUP = A_Q_HEADS // A_KV_HEADS
B_HEADS = 4
B_V_DIM = 2 * HEAD_DIM
A_Q_COLS = A_Q_HEADS * HEAD_DIM
A_KV_COLS = A_KV_HEADS * HEAD_DIM
B_QK_COLS = B_HEADS * 2 * HEAD_DIM
B_V_COLS = B_HEADS * B_V_DIM
OFF_AQ = 0
OFF_AK = OFF_AQ + A_Q_COLS
OFF_AV = OFF_AK + A_KV_COLS
OFF_BQ = OFF_AV + A_KV_COLS
OFF_BK = OFF_BQ + B_QK_COLS
OFF_BV = OFF_BK + B_QK_COLS
IN_COLS = OFF_BV + B_V_COLS
D_FF = 4 * D_MODEL
N_MOD = 6
ROPE_BASE = 10000.0
EPS = 1e-6
LAM_INIT = 0.2
LOG2E = 1.4426950408889634
Q_PRESCALE = HEAD_DIM ** -0.5 * LOG2E
NEG_BIG = -1e30

V7X_VMEM_BYTES = 64 * 1024 * 1024
VMEM_LIMIT = 56 * 1024 * 1024


def _rms(x, axis):
    return x * lax.rsqrt(jnp.mean(x * x, axis=axis, keepdims=True) + EPS)


def _modulated_norm(x, g, scale, shift):
    return (_rms(x, -1) * g) * (1.0 + scale) + shift


def _mod_kernel(cond_ref, w_ref, b_ref, o_ref):
    c = cond_ref[...]
    s = c * jax.nn.sigmoid(c)
    o_ref[...] = jnp.dot(s.astype(BF16), w_ref[...].astype(BF16),
                         preferred_element_type=F32) + b_ref[...]


def _modulation(cond, w_mod, b_mod):
    rows = cond.shape[0]
    n = w_mod.shape[1]
    tn = 1536
    return pl.pallas_call(
        _mod_kernel,
        grid=(n // tn,),
        in_specs=[
            pl.BlockSpec((rows, D_MODEL), lambda j: (0, 0)),
            pl.BlockSpec((D_MODEL, tn), lambda j: (0, j)),
            pl.BlockSpec((1, tn), lambda j: (0, j)),
        ],
        out_specs=pl.BlockSpec((rows, tn), lambda j: (0, j)),
        out_shape=jax.ShapeDtypeStruct((rows, n), F32),
        compiler_params=pltpu.CompilerParams(
            dimension_semantics=("arbitrary",), vmem_limit_bytes=VMEM_LIMIT),
        name="modulation",
    )(cond, w_mod, b_mod.reshape(1, n))


def _rope_t(x, cos, sin):
    x1, x2 = x[:HALF], x[HALF:]
    return jnp.concatenate([x1 * cos - x2 * sin, x1 * sin + x2 * cos], axis=0)


def _proj_kernel(*refs, use_rope, emit_cache, with_past):
    it = iter(refs)
    x_ref, mods_ref, g1_ref, w_ref, gq_ref, gk_ref = (next(it) for _ in range(6))
    cos_ref = sin_ref = None
    if use_rope:
        cos_ref, sin_ref = next(it), next(it)
    if with_past:
        pak_ref, pav_ref, pbk_ref, pbv_ref = (next(it) for _ in range(4))
    qa_ref, ka_ref, va_ref, qb_ref, kb_ref, vb_ref = (next(it) for _ in range(6))
    if emit_cache:
        cak_ref, cav_ref, cbk_ref, cbv_ref = (next(it) for _ in range(4))

    def project():
        x = x_ref[0]
        mods = mods_ref[0]
        h = _modulated_norm(x, g1_ref[...], mods[1:2], mods[0:1])
        t = lax.dot_general(w_ref[...], h.astype(BF16), (((1,), (1,)), ((), ())),
                            preferred_element_type=F32)
        if use_rope:
            cos, sin = cos_ref[...], sin_ref[...]

        def head(off, gain=None):
            blk = t[off:off + HEAD_DIM]
            if gain is not None:
                blk = _rms(blk, 0) * gain
            if use_rope:
                blk = _rope_t(blk, cos, sin)
            return blk

        gq, gk = gq_ref[...], gk_ref[...]
        for i in range(A_Q_HEADS):
            off = OFF_AQ + i * HEAD_DIM
            qa_ref[0, i * HEAD_DIM:(i + 1) * HEAD_DIM, :] = (head(off, gq) * Q_PRESCALE).astype(BF16)
        for i in range(2 * B_HEADS):
            off = OFF_BQ + i * HEAD_DIM
            qb_ref[0, i * HEAD_DIM:(i + 1) * HEAD_DIM, :] = (head(off) * Q_PRESCALE).astype(BF16)

        ak_t = jnp.concatenate([head(OFF_AK + i * HEAD_DIM, gk) for i in range(A_KV_HEADS)], axis=0)
        ak = ak_t.T
        ka_ref[0] = ak.astype(BF16)
        bk_t = jnp.concatenate([head(OFF_BK + i * HEAD_DIM) for i in range(2 * B_HEADS)], axis=0)
        bk = bk_t.T
        kb_ref[0] = bk.astype(BF16)
        av_t = t[OFF_AV:OFF_AV + A_KV_COLS]
        bv_t = t[OFF_BV:OFF_BV + B_V_COLS]
        va_ref[0] = av_t.astype(BF16)
        vb_ref[0] = bv_t.astype(BF16)
        if emit_cache:
            cak_ref[0] = ak
            cbk_ref[0] = bk
            cav_ref[0] = av_t.T
            cbv_ref[0] = bv_t.T

    if not with_past:
        project()
        return

    step = pl.program_id(1)

    @pl.when(step == 0)
    def _():
        ka_ref[0] = pak_ref[0].astype(BF16)
        kb_ref[0] = pbk_ref[0].astype(BF16)
        va_ref[0] = pav_ref[0].T.astype(BF16)
        vb_ref[0] = pbv_ref[0].T.astype(BF16)

    pl.when(step > 0)(project)


def _projection(x, mods, mod_row, g1, w_in_t, gq, gk, rope, emit_cache, tm, past=None):
    b, s, _ = x.shape
    use_rope = rope is not None
    with_past = past is not None
    lead = 1 if with_past else 0
    if with_past:
        assert all(p.shape[1] == tm for p in past), "cached context must fill exactly one key block"
    tok = lambda i: jnp.maximum(i - lead, 0)
    full = lambda shape: pl.BlockSpec(shape, lambda bi, i: (0,) * len(shape))
    in_specs = [
        pl.BlockSpec((1, tm, D_MODEL), lambda bi, i: (bi, tok(i), 0)),
        pl.BlockSpec((1, N_MOD, D_MODEL), lambda bi, i: (mod_row(bi), 0, 0)),
        full((1, D_MODEL)),
        full((IN_COLS, D_MODEL)),
        full((HEAD_DIM, 1)),
        full((HEAD_DIM, 1)),
    ]
    args = [x, mods, g1, w_in_t, gq, gk]
    if use_rope:
        in_specs += [pl.BlockSpec((HALF, tm), lambda bi, i: (0, tok(i)))] * 2
        args += list(rope)
    if with_past:
        in_specs += [pl.BlockSpec((1,) + p.shape[1:], lambda bi, i: (bi, 0, 0)) for p in past]
        args += list(past)
    keys = s + lead * tm
    q_spec = lambda rows: pl.BlockSpec((1, rows, tm), lambda bi, i: (bi, 0, tok(i)))
    fm = lambda rows: pl.BlockSpec((1, rows, tm), lambda bi, i: (bi, 0, i))
    tk = lambda cols: pl.BlockSpec((1, tm, cols), lambda bi, i: (bi, i, 0))
    out_specs = [q_spec(A_Q_COLS), tk(A_KV_COLS), fm(A_KV_COLS),
                 q_spec(B_QK_COLS), tk(B_QK_COLS), fm(B_V_COLS)]
    out_shape = [
        jax.ShapeDtypeStruct((b, A_Q_COLS, s), BF16),
        jax.ShapeDtypeStruct((b, keys, A_KV_COLS), BF16),
        jax.ShapeDtypeStruct((b, A_KV_COLS, keys), BF16),
        jax.ShapeDtypeStruct((b, B_QK_COLS, s), BF16),
        jax.ShapeDtypeStruct((b, keys, B_QK_COLS), BF16),
        jax.ShapeDtypeStruct((b, B_V_COLS, keys), BF16),
    ]
    if emit_cache:
        assert not with_past
        out_specs += [tk(A_KV_COLS), tk(A_KV_COLS), tk(B_QK_COLS), tk(B_V_COLS)]
        out_shape += [
            jax.ShapeDtypeStruct((b, s, A_KV_COLS), F32),
            jax.ShapeDtypeStruct((b, s, A_KV_COLS), F32),
            jax.ShapeDtypeStruct((b, s, B_QK_COLS), F32),
            jax.ShapeDtypeStruct((b, s, B_V_COLS), F32),
        ]
    return pl.pallas_call(
        functools.partial(_proj_kernel, use_rope=use_rope, emit_cache=emit_cache,
                          with_past=with_past),
        grid=(b, s // tm + lead),
        in_specs=in_specs,
        out_specs=out_specs,
        out_shape=out_shape,
        compiler_params=pltpu.CompilerParams(
            dimension_semantics=("arbitrary", "arbitrary"), vmem_limit_bytes=VMEM_LIMIT),
        name="projection_rope" if use_rope else "projection_ctx",
    )(*args)


ONES_ROWS = 16
ATTN_LOOPS = (("a0", "a1", "b0", "b1", "b2", "b3"),)
ATTN_UNROLL = 4


def _attn_kernel(qa_ref, qb_ref, ka_ref, va_ref, kb_ref, vb_ref, lamv_ref, gs_ref,
                 oa_ref, ob_ref, qpad_sc, m_sc, acc_sc, s_sc, bm_sc, *, tq, tk, n_kv, loops, unroll):
    zeros = jnp.zeros((HEAD_DIM, tq), BF16)
    ones = jnp.ones((ONES_ROWS, tk), BF16)

    streams = {}
    lane = 0
    n_a = A_GROUP * tq
    for g in range(A_KV_HEADS):
        for r in range(A_GROUP):
            hd = g * A_GROUP + r
            q = qa_ref[0, hd * HEAD_DIM:(hd + 1) * HEAD_DIM, :]
            pieces = [zeros, zeros]
            pieces[g] = q
            qpad_sc[:, lane + r * tq:lane + (r + 1) * tq] = jnp.concatenate(pieces, axis=0)
        streams["a%d" % g] = (lane, n_a, ka_ref, slice(None), va_ref,
                              slice(g * HEAD_DIM, (g + 1) * HEAD_DIM), HEAD_DIM)
        lane += n_a
    n_b = 2 * tq
    for hd in range(B_HEADS):
        base = hd * B_V_DIM
        q1 = qb_ref[0, base:base + HEAD_DIM, :]
        q2 = qb_ref[0, base + HEAD_DIM:base + 2 * HEAD_DIM, :]
        qpad_sc[:, lane:lane + tq] = jnp.concatenate([q1, zeros], axis=0)
        qpad_sc[:, lane + tq:lane + 2 * tq] = jnp.concatenate([zeros, q2], axis=0)
        streams["b%d" % hd] = (lane, n_b, kb_ref, slice(base, base + B_V_DIM), vb_ref,
                               slice(base, base + B_V_DIM), B_V_DIM)
        lane += n_b

    m_sc[...] = jnp.full(m_sc.shape, NEG_BIG, F32)
    acc_sc[...] = jnp.zeros(acc_sc.shape, F32)

    def block_offset(j):
        if isinstance(j, int):
            return j * tk
        return pl.multiple_of(j * tk, tk)

    def scores(j, slot, unit):
        name, c = unit
        lo, n, k_ref, k_lanes, v_ref, v_rows, dv = streams[name]
        lo = lo + c * tq
        off = block_offset(j)
        kb = k_ref[0, pl.ds(off, tk), k_lanes]
        s = jnp.dot(kb, qpad_sc[:, lo:lo + tq], preferred_element_type=F32)
        s_sc[slot, :, lo:lo + tq] = s
        bm_sc[slot, :, lo:lo + tq] = jnp.max(s, axis=0, keepdims=True)

    def finish(j, slot, unit):
        name, c = unit
        lo, n, k_ref, k_lanes, v_ref, v_rows, dv = streams[name]
        lo = lo + c * tq
        off = block_offset(j)
        m_prev = m_sc[:, lo:lo + tq]
        m_new = jnp.maximum(m_prev, bm_sc[slot, :, lo:lo + tq])
        alpha = jnp.exp2(m_prev - m_new)
        p = jnp.exp2(s_sc[slot, :, lo:lo + tq] - m_new).astype(BF16)
        vext = jnp.concatenate([v_ref[0, v_rows, pl.ds(off, tk)], ones], axis=0)
        rows = dv + ONES_ROWS
        acc_sc[:rows, lo:lo + tq] = alpha * acc_sc[:rows, lo:lo + tq] + jnp.dot(
            vext, p, preferred_element_type=F32)
        m_sc[:, lo:lo + tq] = m_new

    for names in loops:
        units = [(name, c) for name in names for c in range(streams[name][1] // tq)]

        def advance(j, slot, units=units):
            for unit in units:
                scores(j + 1, 1 - slot, unit)
                finish(j, slot, unit)

        def body(i, carry, advance=advance):
            for k in range(unroll):
                advance(unroll * i + k, k % 2)
            return carry

        for unit in units:
            scores(0, 0, unit)
        n_iter = (n_kv - 1) // unroll
        if n_iter > 0:
            lax.fori_loop(0, n_iter, body, 0)
        for j in range(unroll * n_iter, n_kv - 1):
            advance(j, j % 2)
        for unit in units:
            finish(n_kv - 1, (n_kv - 1) % 2, unit)

    def result(name):
        lo, n, _, _, _, _, dv = streams[name]
        return acc_sc[:dv, lo:lo + n] * (1.0 / acc_sc[dv:dv + 1, lo:lo + n])

    for g in range(A_KV_HEADS):
        o = result("a%d" % g)
        for r in range(A_GROUP):
            hd = g * A_GROUP + r
            oa_ref[0, hd * HEAD_DIM:(hd + 1) * HEAD_DIM, :] = o[:, r * tq:(r + 1) * tq].astype(BF16)

    lv = lamv_ref[...]
    lam = (jnp.exp(jnp.sum(lv[0:1] * lv[1:2], axis=-1, keepdims=True))
           - jnp.exp(jnp.sum(lv[2:3] * lv[3:4], axis=-1, keepdims=True)) + LAM_INIT)
    for hd in range(B_HEADS):
        base = hd * B_V_DIM
        o = result("b%d" % hd)
        d = o[:, :tq] - lam * o[:, tq:]
        d = _rms(d, 0) * gs_ref[...] * (1.0 - LAM_INIT)
        ob_ref[0, base:base + B_V_DIM, :] = d.astype(BF16)


def _attention(qa, qb, ka, va, kb, vb, lamv, gs, tq, tk):
    b, _, s = qa.shape
    t = ka.shape[1]
    lanes = A_KV_HEADS * A_GROUP * tq + B_HEADS * 2 * tq
    qspec = pl.BlockSpec((1, A_Q_COLS, tq), lambda bi, i: (bi, 0, i))
    whole = lambda a: pl.BlockSpec((1,) + a.shape[1:], lambda bi, i: (bi, 0, 0))
    full = lambda a: pl.BlockSpec(a.shape, lambda bi, i: (0,) * a.ndim)
    return pl.pallas_call(
        functools.partial(_attn_kernel, tq=tq, tk=tk, n_kv=t // tk, loops=ATTN_LOOPS,
                          unroll=ATTN_UNROLL),
        grid=(b, s // tq),
        in_specs=[qspec, qspec, whole(ka), whole(va), whole(kb), whole(vb), full(lamv), full(gs)],
        out_specs=[qspec, qspec],
        out_shape=[jax.ShapeDtypeStruct((b, A_Q_COLS, s), BF16),
                   jax.ShapeDtypeStruct((b, B_V_COLS, s), BF16)],
        scratch_shapes=[
            pltpu.VMEM((2 * HEAD_DIM, lanes), BF16),
            pltpu.VMEM((1, lanes), F32),
            pltpu.VMEM((B_V_DIM + ONES_ROWS, lanes), F32),
            pltpu.VMEM((2, tk, lanes), F32),
            pltpu.VMEM((2, 1, lanes), F32),
        ],
        compiler_params=pltpu.CompilerParams(
            dimension_semantics=("arbitrary", "arbitrary"), vmem_limit_bytes=VMEM_LIMIT),
        name="attention_t%d" % t,
    )(qa, qb, ka, va, kb, vb, lamv, gs)


def _out_kernel(x_ref, mods_ref, g1_ref, g2_ref, gf_ref, oa_ref, ob_ref, wg_ref, bg_ref,
                wa_ref, wb_ref, wo_ref, w1_ref, w2_ref, y_ref):
    x = x_ref[0]
    mods = mods_ref[0]
    h = _modulated_norm(x, g1_ref[...], mods[1:2], mods[0:1])
    gates = jax.nn.sigmoid(
        jnp.dot(h.astype(BF16), wg_ref[...], preferred_element_type=F32) + bg_ref[...])
    tn = (((0,), (0,)), ((), ()))
    br_a = lax.dot_general(oa_ref[0], wa_ref[...], tn, preferred_element_type=F32)
    br_b = lax.dot_general(ob_ref[0], wb_ref[...], tn, preferred_element_type=F32)
    merged = gates[:, :D_MODEL] * br_a + gates[:, D_MODEL:] * br_b
    x = x + mods[2:3] * jnp.dot(merged.astype(BF16), wo_ref[...], preferred_element_type=F32)
    h2 = _modulated_norm(x, g2_ref[...], mods[4:5], mods[3:4])
    f = jnp.dot(h2.astype(BF16), w1_ref[...], preferred_element_type=F32)
    f = jnp.square(jnp.maximum(f, 0.0))
    x = x + mods[5:6] * jnp.dot(f.astype(BF16), w2_ref[...], preferred_element_type=F32)
    y_ref[0] = _rms(x, -1) * gf_ref[...]


def _output(x, mods, mod_row, g1, g2, gf, oa, ob, wg, bg, wa, wb, wo, w1, w2, tm):
    b, s, _ = x.shape
    full = lambda a: pl.BlockSpec(a.shape, lambda bi, i: (0,) * a.ndim)
    xspec = pl.BlockSpec((1, tm, D_MODEL), lambda bi, i: (bi, i, 0))
    ospec = pl.BlockSpec((1, A_Q_COLS, tm), lambda bi, i: (bi, 0, i))
    return pl.pallas_call(
        _out_kernel,
        grid=(b, s // tm),
        in_specs=[xspec,
                  pl.BlockSpec((1, N_MOD, D_MODEL), lambda bi, i: (mod_row(bi), 0, 0)),
                  full(g1), full(g2), full(gf), ospec, ospec,
                  full(wg), full(bg), full(wa), full(wb), full(wo), full(w1), full(w2)],
        out_specs=xspec,
        out_shape=jax.ShapeDtypeStruct(x.shape, F32),
        compiler_params=pltpu.CompilerParams(
            dimension_semantics=("arbitrary", "arbitrary"), vmem_limit_bytes=VMEM_LIMIT),
        name="output_s%d" % s,
    )(x, mods, g1, g2, gf, oa, ob, wg, bg, wa, wb, wo, w1, w2)


def _rope_tables_t(n_tokens):
    pos = jnp.arange(n_tokens, dtype=jnp.int32)
    row = (pos // GRID_W).astype(F32)
    col = (pos % GRID_W).astype(F32)
    n_freq = HEAD_DIM // 4
    freqs = ROPE_BASE ** (-jnp.arange(n_freq, dtype=F32) / n_freq)
    ang = jnp.concatenate([freqs[:, None] * row[None, :], freqs[:, None] * col[None, :]], axis=0)
    return jnp.cos(ang), jnp.sin(ang)


def kernel(x_prompt, x_sample, cache_a_k, cache_a_v, cache_b_k, cache_b_v, c, c_ctx, w_mod, b_mod, norm1_g, w_in, a_q_norm_g, a_k_norm_g, lam_q1, lam_k1, lam_q2, lam_k2, b_subln_g, w_gate, b_gate, w_br_a, w_br_b, w_out, norm2_g, w_fc1, w_fc2, final_norm_g):
    nb, seq, _ = x_prompt.shape
    db, dseq, _ = x_sample.shape
    past = cache_a_k.shape[2]
    l = 0

    cond = jnp.concatenate([c, c_ctx[None, :]], axis=0)
    ctx_row = db
    rows = -(-cond.shape[0] // 8) * 8
    cond = jnp.pad(cond, ((0, rows - cond.shape[0]), (0, 0)))
    mods = _modulation(cond, w_mod[l], b_mod[l]).reshape(rows, N_MOD, D_MODEL)

    g1 = norm1_g[l].reshape(1, D_MODEL)
    g2 = norm2_g[l].reshape(1, D_MODEL)
    gf = final_norm_g.reshape(1, D_MODEL)
    gq = a_q_norm_g[l].reshape(HEAD_DIM, 1)
    gk = a_k_norm_g[l].reshape(HEAD_DIM, 1)
    gs = b_subln_g[l].reshape(B_V_DIM, 1)
    lamv = jnp.stack([lam_q1[l], lam_k1[l], lam_q2[l], lam_k2[l]], axis=0)
    w_in_t = w_in[l].T.astype(BF16)
    wg = w_gate[l].astype(BF16)
    bg = b_gate[l].reshape(1, -1)
    wa = w_br_a[l].astype(BF16)
    wb = w_br_b[l].astype(BF16)
    wo = w_out[l].astype(BF16)
    w1 = w_fc1[l].astype(BF16)
    w2 = w_fc2[l].astype(BF16)

    ctx_mod = lambda bi: ctx_row
    lat_mod = lambda bi: bi

    tm_c = min(seq, 512)
    qa, ka, va, qb, kb, vb, new_ak, new_av, new_bk, new_bv = _projection(
        x_prompt, mods, ctx_mod, g1, w_in_t, gq, gk, None, True, tm_c)
    oa, ob = _attention(qa, qb, ka, va, kb, vb, lamv, gs, tq=min(seq, 256), tk=min(seq, 512))
    y_prompt = _output(x_prompt, mods, ctx_mod, g1, g2, gf, oa, ob, wg, bg, wa, wb, wo, w1, w2, tm_c)

    rope = _rope_tables_t(dseq)
    cached = (cache_a_k[:, l].reshape(db, past, A_KV_COLS), cache_a_v[:, l].reshape(db, past, A_KV_COLS),
              cache_b_k[:, l].reshape(db, past, B_QK_COLS), cache_b_v[:, l].reshape(db, past, B_V_COLS))
    qa, ka, va, qb, kb, vb = _projection(
        x_sample, mods, lat_mod, g1, w_in_t, gq, gk, rope, False, past, past=cached)
    oa, ob = _attention(qa, qb, ka, va, kb, vb, lamv, gs, tq=256, tk=512)
    y_sample = _output(x_sample, mods, lat_mod, g1, g2, gf, oa, ob, wg, bg, wa, wb, wo, w1, w2, 512)

    return (y_prompt, y_sample,
            new_ak.reshape(nb, 1, seq, A_KV_HEADS, HEAD_DIM),
            new_av.reshape(nb, 1, seq, A_KV_HEADS, HEAD_DIM),
            new_bk.reshape(nb, 1, seq, B_HEADS, 2, HEAD_DIM),
            new_bv.reshape(nb, 1, seq, B_HEADS, B_V_DIM))
```

```python
import functools

import jax
import jax.numpy as jnp
from jax import lax
from jax.experimental import pallas as pl
from jax.experimental.pallas import tpu as pltpu

F32 = jnp.float32
BF16 = jnp.bfloat16

D_MODEL = 1024
HEAD_DIM = 64
HALF = HEAD_DIM // 2
GRID_W = 64
A_Q_HEADS = 8
A_KV_HEADS = 2
A_GROUP = A_Q_HEADS // A_KV_HEADS
B_HEADS = 4
B_V_DIM = 2 * HEAD_DIM
A_Q_COLS = A_Q_HEADS * HEAD_DIM
A_KV_COLS = A_KV_HEADS * HEAD_DIM
B_QK_COLS = B_HEADS * 2 * HEAD_DIM
B_V_COLS = B_HEADS * B_V_DIM
OFF_AQ = 0
OFF_AK = OFF_AQ + A_Q_COLS
OFF_AV = OFF_AK + A_KV_COLS
OFF_BQ = OFF_AV + A_KV_COLS
OFF_BK = OFF_BQ + B_QK_COLS
OFF_BV = OFF_BK + B_QK_COLS
IN_COLS = OFF_BV + B_V_COLS
D_FF = 4 * D_MODEL
N_MOD = 6
ROPE_BASE = 10000.0
EPS = 1e-6
LAM_INIT = 0.2
LOG2E = 1.4426950408889634
Q_PRESCALE = HEAD_DIM ** -0.5 * LOG2E

V7X_VMEM_BYTES = 64 * 1024 * 1024
VMEM_LIMIT = 56 * 1024 * 1024


def _rms(x, axis):
    return x * lax.rsqrt(jnp.mean(x * x, axis=axis, keepdims=True) + EPS)


def _modulated_norm(x, g, scale, shift):
    return (_rms(x, -1) * g) * (1.0 + scale) + shift


def _mod_kernel(cond_ref, w_ref, b_ref, o_ref):
    c = cond_ref[...]
    s = c * jax.nn.sigmoid(c)
    o_ref[...] = jnp.dot(s.astype(BF16), w_ref[...].astype(BF16),
                         preferred_element_type=F32) + b_ref[...]


def _modulation(cond, w_mod, b_mod):
    rows = cond.shape[0]
    n = w_mod.shape[1]
    tn = 1536
    return pl.pallas_call(
        _mod_kernel,
        grid=(n // tn,),
        in_specs=[
            pl.BlockSpec((rows, D_MODEL), lambda j: (0, 0)),
            pl.BlockSpec((D_MODEL, tn), lambda j: (0, j)),
            pl.BlockSpec((1, tn), lambda j: (0, j)),
        ],
        out_specs=pl.BlockSpec((rows, tn), lambda j: (0, j)),
        out_shape=jax.ShapeDtypeStruct((rows, n), F32),
        compiler_params=pltpu.CompilerParams(
            dimension_semantics=("arbitrary",), vmem_limit_bytes=VMEM_LIMIT),
        name="modulation",
    )(cond, w_mod, b_mod.reshape(1, n))


def _rope_t(x, cos, sin):
    x1, x2 = x[:HALF], x[HALF:]
    return jnp.concatenate([x1 * cos - x2 * sin, x1 * sin + x2 * cos], axis=0)


def _proj_kernel(*refs, use_rope, emit_cache, with_past):
    it = iter(refs)
    x_ref, mods_ref, g1_ref, w_ref, gq_ref, gk_ref = (next(it) for _ in range(6))
    cos_ref = sin_ref = None
    if use_rope:
        cos_ref, sin_ref = next(it), next(it)
    if with_past:
        pak_ref, pav_ref, pbk_ref, pbv_ref = (next(it) for _ in range(4))
    qa_ref, ka_ref, va_ref, qb_ref, kb_ref, vb_ref = (next(it) for _ in range(6))
    if emit_cache:
        cak_ref, cav_ref, cbk_ref, cbv_ref = (next(it) for _ in range(4))

    def project():
        x = x_ref[0]
        mods = mods_ref[0]
        h = _modulated_norm(x, g1_ref[...], mods[1:2], mods[0:1])
        t = lax.dot_general(w_ref[...], h.astype(BF16), (((1,), (1,)), ((), ())),
                            preferred_element_type=F32)
        if use_rope:
            cos, sin = cos_ref[...], sin_ref[...]

        def head(off, gain=None):
            blk = t[off:off + HEAD_DIM]
            if gain is not None:
                blk = _rms(blk, 0) * gain
            if use_rope:
                blk = _rope_t(blk, cos, sin)
            return blk

        gq, gk = gq_ref[...], gk_ref[...]
        for i in range(A_Q_HEADS):
            off = OFF_AQ + i * HEAD_DIM
            qa_ref[0, i * HEAD_DIM:(i + 1) * HEAD_DIM, :] = (head(off, gq) * Q_PRESCALE).astype(BF16)
        for i in range(2 * B_HEADS):
            off = OFF_BQ + i * HEAD_DIM
            qb_ref[0, i * HEAD_DIM:(i + 1) * HEAD_DIM, :] = (head(off) * Q_PRESCALE).astype(BF16)

        ak_t = jnp.concatenate([head(OFF_AK + i * HEAD_DIM, gk) for i in range(A_KV_HEADS)], axis=0)
        ak = ak_t.T
        ka_ref[0] = ak.astype(BF16)
        bk_t = jnp.concatenate([head(OFF_BK + i * HEAD_DIM) for i in range(2 * B_HEADS)], axis=0)
        bk = bk_t.T
        kb_ref[0] = bk.astype(BF16)
        av_t = t[OFF_AV:OFF_AV + A_KV_COLS]
        bv_t = t[OFF_BV:OFF_BV + B_V_COLS]
        va_ref[0] = av_t.astype(BF16)
        vb_ref[0] = bv_t.astype(BF16)
        if emit_cache:
            cak_ref[0] = ak
            cbk_ref[0] = bk
            cav_ref[0] = av_t.T
            cbv_ref[0] = bv_t.T

    if not with_past:
        project()
        return

    step = pl.program_id(1)

    @pl.when(step == 0)
    def _():
        ka_ref[0] = pak_ref[0].astype(BF16)
        kb_ref[0] = pbk_ref[0].astype(BF16)
        va_ref[0] = pav_ref[0].T.astype(BF16)
        vb_ref[0] = pbv_ref[0].T.astype(BF16)

    pl.when(step > 0)(project)


def _projection(x, mods, mod_row, g1, w_in_t, gq, gk, rope, emit_cache, tm, past=None):
    b, s, _ = x.shape
    use_rope = rope is not None
    with_past = past is not None
    lead = 1 if with_past else 0
    if with_past:
        assert all(p.shape[1] == tm for p in past), "cached context must fill exactly one key block"
    tok = lambda i: jnp.maximum(i - lead, 0)
    full = lambda shape: pl.BlockSpec(shape, lambda bi, i: (0,) * len(shape))
    in_specs = [
        pl.BlockSpec((1, tm, D_MODEL), lambda bi, i: (bi, tok(i), 0)),
        pl.BlockSpec((1, N_MOD, D_MODEL), lambda bi, i: (mod_row(bi), 0, 0)),
        full((1, D_MODEL)),
        full((IN_COLS, D_MODEL)),
        full((HEAD_DIM, 1)),
        full((HEAD_DIM, 1)),
    ]
    args = [x, mods, g1, w_in_t, gq, gk]
    if use_rope:
        in_specs += [pl.BlockSpec((HALF, tm), lambda bi, i: (0, tok(i)))] * 2
        args += list(rope)
    if with_past:
        in_specs += [pl.BlockSpec((1,) + p.shape[1:], lambda bi, i: (bi, 0, 0)) for p in past]
        args += list(past)
    keys = s + lead * tm
    q_spec = lambda rows: pl.BlockSpec((1, rows, tm), lambda bi, i: (bi, 0, tok(i)))
    fm = lambda rows: pl.BlockSpec((1, rows, tm), lambda bi, i: (bi, 0, i))
    tk = lambda cols: pl.BlockSpec((1, tm, cols), lambda bi, i: (bi, i, 0))
    out_specs = [q_spec(A_Q_COLS), tk(A_KV_COLS), fm(A_KV_COLS),
                 q_spec(B_QK_COLS), tk(B_QK_COLS), fm(B_V_COLS)]
    out_shape = [
        jax.ShapeDtypeStruct((b, A_Q_COLS, s), BF16),
        jax.ShapeDtypeStruct((b, keys, A_KV_COLS), BF16),
        jax.ShapeDtypeStruct((b, A_KV_COLS, keys), BF16),
        jax.ShapeDtypeStruct((b, B_QK_COLS, s), BF16),
        jax.ShapeDtypeStruct((b, keys, B_QK_COLS), BF16),
        jax.ShapeDtypeStruct((b, B_V_COLS, keys), BF16),
    ]
    if emit_cache:
        assert not with_past
        out_specs += [tk(A_KV_COLS), tk(A_KV_COLS), tk(B_QK_COLS), tk(B_V_COLS)]
        out_shape += [
            jax.ShapeDtypeStruct((b, s, A_KV_COLS), F32),
            jax.ShapeDtypeStruct((b, s, A_KV_COLS), F32),
            jax.ShapeDtypeStruct((b, s, B_QK_COLS), F32),
            jax.ShapeDtypeStruct((b, s, B_V_COLS), F32),
        ]
    return pl.pallas_call(
        functools.partial(_proj_kernel, use_rope=use_rope, emit_cache=emit_cache,
                          with_past=with_past),
        grid=(b, s // tm + lead),
        in_specs=in_specs,
        out_specs=out_specs,
        out_shape=out_shape,
        compiler_params=pltpu.CompilerParams(
            dimension_semantics=("arbitrary", "arbitrary"), vmem_limit_bytes=VMEM_LIMIT),
        name="projection_rope" if use_rope else "projection_ctx",
    )(*args)


ONES_ROWS = 16
ATTN_UNITS = A_Q_HEADS + 2 * B_HEADS
ATTN_UNROLL = 8


def _attn_kernel(qa_ref, qb_ref, ka_ref, va_ref, kb_ref, vb_ref, lamv_ref, gs_ref,
                 oa_ref, ob_ref, qpad_sc, m_sc, acc_sc, s_sc, bm_sc, *, tq, tk, n_kv, unroll):
    zeros = jnp.zeros((HEAD_DIM, tq), BF16)
    ones = jnp.ones((ONES_ROWS, tk), BF16)

    units = []
    for hd in range(A_Q_HEADS):
        g = hd // A_GROUP
        pieces = [zeros, zeros]
        pieces[g] = qa_ref[0, hd * HEAD_DIM:(hd + 1) * HEAD_DIM, :]
        qpad_sc[len(units)] = jnp.concatenate(pieces, axis=0)
        units.append((ka_ref, slice(None), va_ref, slice(g * HEAD_DIM, (g + 1) * HEAD_DIM), HEAD_DIM))
    for hd in range(B_HEADS):
        base = hd * B_V_DIM
        for c in range(2):
            pieces = [zeros, zeros]
            pieces[c] = qb_ref[0, base + c * HEAD_DIM:base + (c + 1) * HEAD_DIM, :]
            qpad_sc[len(units)] = jnp.concatenate(pieces, axis=0)
            units.append((kb_ref, slice(base, base + B_V_DIM), vb_ref, slice(base, base + B_V_DIM),
                          B_V_DIM))
    first_b_unit = A_Q_HEADS

    def block_offset(j):
        if isinstance(j, int):
            return j * tk
        return pl.multiple_of(j * tk, tk)

    def scores(j, slot, u):
        k_ref, k_lanes, _, _, _ = units[u]
        kb = k_ref[0, pl.ds(block_offset(j), tk), k_lanes]
        s = jnp.dot(kb, qpad_sc[u], preferred_element_type=F32)
        s_sc[slot, u] = s
        bm_sc[slot, u] = jnp.max(s, axis=0, keepdims=True)

    def finish(j, slot, u, first=False):
        _, _, v_ref, v_rows, dv = units[u]
        vext = jnp.concatenate([v_ref[0, v_rows, pl.ds(block_offset(j), tk)], ones], axis=0)
        rows = dv + ONES_ROWS
        if first:
            m_new = bm_sc[slot, u]
        else:
            m_prev = m_sc[u]
            m_new = jnp.maximum(m_prev, bm_sc[slot, u])
        p = jnp.exp2(s_sc[slot, u] - m_new).astype(BF16)
        pv = jnp.dot(vext, p, preferred_element_type=F32)
        if first:
            acc_sc[u, :rows] = pv
        else:
            acc_sc[u, :rows] = jnp.exp2(m_prev - m_new) * acc_sc[u, :rows] + pv
        m_sc[u] = m_new

    assert unroll % 2 == 0
    all_units = range(len(units))

    def advance(j, slot, first=False):
        for u in all_units:
            scores(j + 1, 1 - slot, u)
            finish(j, slot, u, first)

    def body(i, carry):
        for k in range(unroll):
            advance(1 + unroll * i + k, (1 + k) % 2)
        return carry

    for u in all_units:
        scores(0, 0, u)
    if n_kv > 1:
        advance(0, 0, first=True)
    n_iter = max(n_kv - 2, 0) // unroll
    if n_iter > 0:
        lax.fori_loop(0, n_iter, body, 0)
    for j in range(1 + unroll * n_iter, n_kv - 1):
        advance(j, j % 2)
    for u in all_units:
        finish(n_kv - 1, (n_kv - 1) % 2, u, first=(n_kv == 1))

    def result(u):
        dv = units[u][4]
        return acc_sc[u, :dv] * (1.0 / acc_sc[u, dv:dv + 1])

    for hd in range(A_Q_HEADS):
        oa_ref[0, hd * HEAD_DIM:(hd + 1) * HEAD_DIM, :] = result(hd).astype(BF16)

    lv = lamv_ref[...]
    lam = (jnp.exp(jnp.sum(lv[0:1] * lv[1:2], axis=-1, keepdims=True))
           - jnp.exp(jnp.sum(lv[2:3] * lv[3:4], axis=-1, keepdims=True)) + LAM_INIT)
    gain = gs_ref[...] * (1.0 - LAM_INIT)
    for hd in range(B_HEADS):
        base = hd * B_V_DIM
        u = first_b_unit + 2 * hd
        d = result(u) - lam * result(u + 1)
        ob_ref[0, base:base + B_V_DIM, :] = (_rms(d, 0) * gain).astype(BF16)


def _attention(qa, qb, ka, va, kb, vb, lamv, gs, tq, tk):
    b, _, s = qa.shape
    t = ka.shape[1]
    qspec = pl.BlockSpec((1, A_Q_COLS, tq), lambda bi, i: (bi, 0, i))
    whole = lambda a: pl.BlockSpec((1,) + a.shape[1:], lambda bi, i: (bi, 0, 0))
    full = lambda a: pl.BlockSpec(a.shape, lambda bi, i: (0,) * a.ndim)
    return pl.pallas_call(
        functools.partial(_attn_kernel, tq=tq, tk=tk, n_kv=t // tk, unroll=ATTN_UNROLL),
        grid=(b, s // tq),
        in_specs=[qspec, qspec, whole(ka), whole(va), whole(kb), whole(vb), full(lamv), full(gs)],
        out_specs=[qspec, qspec],
        out_shape=[jax.ShapeDtypeStruct((b, A_Q_COLS, s), BF16),
                   jax.ShapeDtypeStruct((b, B_V_COLS, s), BF16)],
        scratch_shapes=[
            pltpu.VMEM((ATTN_UNITS, 2 * HEAD_DIM, tq), BF16),
            pltpu.VMEM((ATTN_UNITS, 1, tq), F32),
            pltpu.VMEM((ATTN_UNITS, B_V_DIM + ONES_ROWS, tq), F32),
            pltpu.VMEM((2, ATTN_UNITS, tk, tq), F32),
            pltpu.VMEM((2, ATTN_UNITS, 1, tq), F32),
        ],
        compiler_params=pltpu.CompilerParams(
            dimension_semantics=("arbitrary", "arbitrary"), vmem_limit_bytes=VMEM_LIMIT),
        name="attention_t%d" % t,
    )(qa, qb, ka, va, kb, vb, lamv, gs)


def _out_kernel(x_ref, mods_ref, g1_ref, g2_ref, gf_ref, oa_ref, ob_ref, wg_ref, bg_ref,
                wa_ref, wb_ref, wo_ref, w1_ref, w2_ref, y_ref):
    x = x_ref[0]
    mods = mods_ref[0]
    h = _modulated_norm(x, g1_ref[...], mods[1:2], mods[0:1])
    gates = jax.nn.sigmoid(
        jnp.dot(h.astype(BF16), wg_ref[...], preferred_element_type=F32) + bg_ref[...])
    tn = (((0,), (0,)), ((), ()))
    br_a = lax.dot_general(oa_ref[0], wa_ref[...], tn, preferred_element_type=F32)
    br_b = lax.dot_general(ob_ref[0], wb_ref[...], tn, preferred_element_type=F32)
    merged = gates[:, :D_MODEL] * br_a + gates[:, D_MODEL:] * br_b
    x = x + mods[2:3] * jnp.dot(merged.astype(BF16), wo_ref[...], preferred_element_type=F32)
    h2 = _modulated_norm(x, g2_ref[...], mods[4:5], mods[3:4])
    f = jnp.dot(h2.astype(BF16), w1_ref[...], preferred_element_type=F32)
    f = jnp.square(jnp.maximum(f, 0.0))
    x = x + mods[5:6] * jnp.dot(f.astype(BF16), w2_ref[...], preferred_element_type=F32)
    y_ref[0] = _rms(x, -1) * gf_ref[...]


def _output(x, mods, mod_row, g1, g2, gf, oa, ob, wg, bg, wa, wb, wo, w1, w2, tm):
    b, s, _ = x.shape
    full = lambda a: pl.BlockSpec(a.shape, lambda bi, i: (0,) * a.ndim)
    xspec = pl.BlockSpec((1, tm, D_MODEL), lambda bi, i: (bi, i, 0))
    ospec = pl.BlockSpec((1, A_Q_COLS, tm), lambda bi, i: (bi, 0, i))
    return pl.pallas_call(
        _out_kernel,
        grid=(b, s // tm),
        in_specs=[xspec,
                  pl.BlockSpec((1, N_MOD, D_MODEL), lambda bi, i: (mod_row(bi), 0, 0)),
                  full(g1), full(g2), full(gf), ospec, ospec,
                  full(wg), full(bg), full(wa), full(wb), full(wo), full(w1), full(w2)],
        out_specs=xspec,
        out_shape=jax.ShapeDtypeStruct(x.shape, F32),
        compiler_params=pltpu.CompilerParams(
            dimension_semantics=("arbitrary", "arbitrary"), vmem_limit_bytes=VMEM_LIMIT),
        name="output_s%d" % s,
    )(x, mods, g1, g2, gf, oa, ob, wg, bg, wa, wb, wo, w1, w2)


def _rope_tables_t(n_tokens):
    pos = jnp.arange(n_tokens, dtype=jnp.int32)
    row = (pos // GRID_W).astype(F32)
    col = (pos % GRID_W).astype(F32)
    n_freq = HEAD_DIM // 4
    freqs = ROPE_BASE ** (-jnp.arange(n_freq, dtype=F32) / n_freq)
    ang = jnp.concatenate([freqs[:, None] * row[None, :], freqs[:, None] * col[None, :]], axis=0)
    return jnp.cos(ang), jnp.sin(ang)


def kernel(x_prompt, x_sample, cache_a_k, cache_a_v, cache_b_k, cache_b_v, c, c_ctx, w_mod, b_mod, norm1_g, w_in, a_q_norm_g, a_k_norm_g, lam_q1, lam_k1, lam_q2, lam_k2, b_subln_g, w_gate, b_gate, w_br_a, w_br_b, w_out, norm2_g, w_fc1, w_fc2, final_norm_g):
    nb, seq, _ = x_prompt.shape
    db, dseq, _ = x_sample.shape
    past = cache_a_k.shape[2]
    l = 0

    cond = jnp.concatenate([c, c_ctx[None, :]], axis=0)
    ctx_row = db
    rows = -(-cond.shape[0] // 8) * 8
    cond = jnp.pad(cond, ((0, rows - cond.shape[0]), (0, 0)))
    mods = _modulation(cond, w_mod[l], b_mod[l]).reshape(rows, N_MOD, D_MODEL)

    g1 = norm1_g[l].reshape(1, D_MODEL)
    g2 = norm2_g[l].reshape(1, D_MODEL)
    gf = final_norm_g.reshape(1, D_MODEL)
    gq = a_q_norm_g[l].reshape(HEAD_DIM, 1)
    gk = a_k_norm_g[l].reshape(HEAD_DIM, 1)
    gs = b_subln_g[l].reshape(B_V_DIM, 1)
    lamv = jnp.stack([lam_q1[l], lam_k1[l], lam_q2[l], lam_k2[l]], axis=0)
    w_in_t = w_in[l].T.astype(BF16)
    wg = w_gate[l].astype(BF16)
    bg = b_gate[l].reshape(1, -1)
    wa = w_br_a[l].astype(BF16)
    wb = w_br_b[l].astype(BF16)
    wo = w_out[l].astype(BF16)
    w1 = w_fc1[l].astype(BF16)
    w2 = w_fc2[l].astype(BF16)

    ctx_mod = lambda bi: ctx_row
    lat_mod = lambda bi: bi

    tm_c = min(seq, 512)
    qa, ka, va, qb, kb, vb, new_ak, new_av, new_bk, new_bv = _projection(
        x_prompt, mods, ctx_mod, g1, w_in_t, gq, gk, None, True, tm_c)
    oa, ob = _attention(qa, qb, ka, va, kb, vb, lamv, gs, tq=min(seq, 256), tk=min(seq, 512))
    y_prompt = _output(x_prompt, mods, ctx_mod, g1, g2, gf, oa, ob, wg, bg, wa, wb, wo, w1, w2, tm_c)

    rope = _rope_tables_t(dseq)
    cached = (cache_a_k[:, l].reshape(db, past, A_KV_COLS), cache_a_v[:, l].reshape(db, past, A_KV_COLS),
              cache_b_k[:, l].reshape(db, past, B_QK_COLS), cache_b_v[:, l].reshape(db, past, B_V_COLS))
    qa, ka, va, qb, kb, vb = _projection(
        x_sample, mods, lat_mod, g1, w_in_t, gq, gk, rope, False, past, past=cached)
    oa, ob = _attention(qa, qb, ka, va, kb, vb, lamv, gs, tq=256, tk=256)
    y_sample = _output(x_sample, mods, lat_mod, g1, g2, gf, oa, ob, wg, bg, wa, wb, wo, w1, w2, 512)

    return (y_prompt, y_sample,
            new_ak.reshape(nb, 1, seq, A_KV_HEADS, HEAD_DIM),
            new_av.reshape(nb, 1, seq, A_KV_HEADS, HEAD_DIM),
            new_bk.reshape(nb, 1, seq, B_HEADS, 2, HEAD_DIM),
            new_bv.reshape(nb, 1, seq, B_HEADS, B_V_DIM))
```

```python
import functools

import jax
import jax.numpy as jnp
from jax import lax
from jax.experimental import pallas as pl
from jax.experimental.pallas import tpu as pltpu

F32 = jnp.float32
BF16 = jnp.bfloat16

D_MODEL = 1024
HEAD_DIM = 64
HALF = HEAD_DIM // 2
GRID_W = 64
A_Q_HEADS = 8
A_KV_HEADS = 2
A_GROUP = A_Q_HEADS // A_KV_HEADS
B_HEADS = 4
B_V_DIM = 2 * HEAD_DIM
A_Q_COLS = A_Q_HEADS * HEAD_DIM
A_KV_COLS = A_KV_HEADS * HEAD_DIM
B_QK_COLS = B_HEADS * 2 * HEAD_DIM
B_V_COLS = B_HEADS * B_V_DIM
OFF_AQ = 0
OFF_AK = OFF_AQ + A_Q_COLS
OFF_AV = OFF_AK + A_KV_COLS
OFF_BQ = OFF_AV + A_KV_COLS
OFF_BK = OFF_BQ + B_QK_COLS
OFF_BV = OFF_BK + B_QK_COLS
IN_COLS = OFF_BV + B_V_COLS
D_FF = 4 * D_MODEL
N_MOD = 6
ROPE_BASE = 10000.0
EPS = 1e-6
LAM_INIT = 0.2
LOG2E = 1.4426950408889634
Q_PRESCALE = HEAD_DIM ** -0.5 * LOG2E

V7X_VMEM_BYTES = 64 * 1024 * 1024
VMEM_LIMIT = 56 * 1024 * 1024


def _rms(x, axis):
    return x * lax.rsqrt(jnp.mean(x * x, axis=axis, keepdims=True) + EPS)


def _modulated_norm(x, g, scale, shift):
    return (_rms(x, -1) * g) * (1.0 + scale) + shift


def _mod_kernel(cond_ref, w_ref, b_ref, o_ref):
    c = cond_ref[...]
    s = c * jax.nn.sigmoid(c)
    o_ref[...] = jnp.dot(s.astype(BF16), w_ref[...].astype(BF16),
                         preferred_element_type=F32) + b_ref[...]


def _modulation(cond, w_mod, b_mod):
    rows = cond.shape[0]
    n = w_mod.shape[1]
    tn = 1536
    return pl.pallas_call(
        _mod_kernel,
        grid=(n // tn,),
        in_specs=[
            pl.BlockSpec((rows, D_MODEL), lambda j: (0, 0)),
            pl.BlockSpec((D_MODEL, tn), lambda j: (0, j)),
            pl.BlockSpec((1, tn), lambda j: (0, j)),
        ],
        out_specs=pl.BlockSpec((rows, tn), lambda j: (0, j)),
        out_shape=jax.ShapeDtypeStruct((rows, n), F32),
        compiler_params=pltpu.CompilerParams(
            dimension_semantics=("arbitrary",), vmem_limit_bytes=VMEM_LIMIT),
        name="modulation",
    )(cond, w_mod, b_mod.reshape(1, n))


def _rope_t(x, cos, sin):
    x1, x2 = x[:HALF], x[HALF:]
    return jnp.concatenate([x1 * cos - x2 * sin, x1 * sin + x2 * cos], axis=0)


def _proj_kernel(*refs, use_rope, emit_cache, with_past):
    it = iter(refs)
    x_ref, mods_ref, g1_ref, w_ref, gq_ref, gk_ref = (next(it) for _ in range(6))
    cos_ref = sin_ref = None
    if use_rope:
        cos_ref, sin_ref = next(it), next(it)
    if with_past:
        pak_ref, pav_ref, pbk_ref, pbv_ref = (next(it) for _ in range(4))
    qa_ref, ka_ref, va_ref, qb_ref, kb_ref, vb_ref = (next(it) for _ in range(6))
    if emit_cache:
        cak_ref, cav_ref, cbk_ref, cbv_ref = (next(it) for _ in range(4))

    def project():
        x = x_ref[0]
        mods = mods_ref[0]
        h = _modulated_norm(x, g1_ref[...], mods[1:2], mods[0:1])
        t = lax.dot_general(w_ref[...], h.astype(BF16), (((1,), (1,)), ((), ())),
                            preferred_element_type=F32)
        if use_rope:
            cos, sin = cos_ref[...], sin_ref[...]

        def head(off, gain=None):
            blk = t[off:off + HEAD_DIM]
            if gain is not None:
                blk = _rms(blk, 0) * gain
            if use_rope:
                blk = _rope_t(blk, cos, sin)
            return blk

        gq, gk = gq_ref[...], gk_ref[...]
        for i in range(A_Q_HEADS):
            off = OFF_AQ + i * HEAD_DIM
            qa_ref[0, i * HEAD_DIM:(i + 1) * HEAD_DIM, :] = (head(off, gq) * Q_PRESCALE).astype(BF16)
        for i in range(2 * B_HEADS):
            off = OFF_BQ + i * HEAD_DIM
            qb_ref[0, i * HEAD_DIM:(i + 1) * HEAD_DIM, :] = (head(off) * Q_PRESCALE).astype(BF16)

        ak_t = jnp.concatenate([head(OFF_AK + i * HEAD_DIM, gk) for i in range(A_KV_HEADS)], axis=0)
        ak = ak_t.T
        ka_ref[0] = ak.astype(BF16)
        bk_t = jnp.concatenate([head(OFF_BK + i * HEAD_DIM) for i in range(2 * B_HEADS)], axis=0)
        bk = bk_t.T
        kb_ref[0] = bk.astype(BF16)
        av_t = t[OFF_AV:OFF_AV + A_KV_COLS]
        bv_t = t[OFF_BV:OFF_BV + B_V_COLS]
        va_ref[0] = av_t.astype(BF16)
        vb_ref[0] = bv_t.astype(BF16)
        if emit_cache:
            cak_ref[0] = ak
            cbk_ref[0] = bk
            cav_ref[0] = av_t.T
            cbv_ref[0] = bv_t.T

    if not with_past:
        project()
        return

    step = pl.program_id(1)

    @pl.when(step == 0)
    def _():
        ka_ref[0] = pak_ref[0].astype(BF16)
        kb_ref[0] = pbk_ref[0].astype(BF16)
        va_ref[0] = pav_ref[0].T.astype(BF16)
        vb_ref[0] = pbv_ref[0].T.astype(BF16)

    pl.when(step > 0)(project)


def _projection(x, mods, mod_row, g1, w_in_t, gq, gk, rope, emit_cache, tm, past=None):
    b, s, _ = x.shape
    use_rope = rope is not None
    with_past = past is not None
    lead = 1 if with_past else 0
    if with_past:
        assert all(p.shape[1] == tm for p in past), "cached context must fill exactly one key block"
    tok = lambda i: jnp.maximum(i - lead, 0)
    full = lambda shape: pl.BlockSpec(shape, lambda bi, i: (0,) * len(shape))
    in_specs = [
        pl.BlockSpec((1, tm, D_MODEL), lambda bi, i: (bi, tok(i), 0)),
        pl.BlockSpec((1, N_MOD, D_MODEL), lambda bi, i: (mod_row(bi), 0, 0)),
        full((1, D_MODEL)),
        full((IN_COLS, D_MODEL)),
        full((HEAD_DIM, 1)),
        full((HEAD_DIM, 1)),
    ]
    args = [x, mods, g1, w_in_t, gq, gk]
    if use_rope:
        in_specs += [pl.BlockSpec((HALF, tm), lambda bi, i: (0, tok(i)))] * 2
        args += list(rope)
    if with_past:
        in_specs += [pl.BlockSpec((1,) + p.shape[1:], lambda bi, i: (bi, 0, 0)) for p in past]
        args += list(past)
    keys = s + lead * tm
    q_spec = lambda rows: pl.BlockSpec((1, rows, tm), lambda bi, i: (bi, 0, tok(i)))
    fm = lambda rows: pl.BlockSpec((1, rows, tm), lambda bi, i: (bi, 0, i))
    tk = lambda cols: pl.BlockSpec((1, tm, cols), lambda bi, i: (bi, i, 0))
    out_specs = [q_spec(A_Q_COLS), tk(A_KV_COLS), fm(A_KV_COLS),
                 q_spec(B_QK_COLS), tk(B_QK_COLS), fm(B_V_COLS)]
    out_shape = [
        jax.ShapeDtypeStruct((b, A_Q_COLS, s), BF16),
        jax.ShapeDtypeStruct((b, keys, A_KV_COLS), BF16),
        jax.ShapeDtypeStruct((b, A_KV_COLS, keys), BF16),
        jax.ShapeDtypeStruct((b, B_QK_COLS, s), BF16),
        jax.ShapeDtypeStruct((b, keys, B_QK_COLS), BF16),
        jax.ShapeDtypeStruct((b, B_V_COLS, keys), BF16),
    ]
    if emit_cache:
        assert not with_past
        out_specs += [tk(A_KV_COLS), tk(A_KV_COLS), tk(B_QK_COLS), tk(B_V_COLS)]
        out_shape += [
            jax.ShapeDtypeStruct((b, s, A_KV_COLS), F32),
            jax.ShapeDtypeStruct((b, s, A_KV_COLS), F32),
            jax.ShapeDtypeStruct((b, s, B_QK_COLS), F32),
            jax.ShapeDtypeStruct((b, s, B_V_COLS), F32),
        ]
    return pl.pallas_call(
        functools.partial(_proj_kernel, use_rope=use_rope, emit_cache=emit_cache,
                          with_past=with_past),
        grid=(b, s // tm + lead),
        in_specs=in_specs,
        out_specs=out_specs,
        out_shape=out_shape,
        compiler_params=pltpu.CompilerParams(
            dimension_semantics=("arbitrary", "arbitrary"), vmem_limit_bytes=VMEM_LIMIT),
        name="projection_rope" if use_rope else "projection_ctx",
    )(*args)


ONES_ROWS = 16
ATTN_UNITS = A_Q_HEADS + 2 * B_HEADS
ATTN_UNROLL = 8


def _attn_kernel(qa_ref, qb_ref, ka_ref, va_ref, kb_ref, vb_ref, lamv_ref, gs_ref,
                 oa_ref, ob_ref, qpad_sc, m_sc, acc_sc, s_sc, bm_sc, *, tq, tk, n_kv, unroll, n_tiles):
    zeros = jnp.zeros((HEAD_DIM, tq), BF16)
    ones = jnp.ones((ONES_ROWS, tk), BF16)

    units = []
    for hd in range(A_Q_HEADS):
        g = hd // A_GROUP
        units.append((qa_ref, slice(hd * HEAD_DIM, (hd + 1) * HEAD_DIM), g,
                      ka_ref, slice(None), va_ref, slice(g * HEAD_DIM, (g + 1) * HEAD_DIM), HEAD_DIM))
    for hd in range(B_HEADS):
        base = hd * B_V_DIM
        for c in range(2):
            units.append((qb_ref, slice(base + c * HEAD_DIM, base + (c + 1) * HEAD_DIM), c,
                          kb_ref, slice(base, base + B_V_DIM), vb_ref, slice(base, base + B_V_DIM),
                          B_V_DIM))
    first_b_unit = A_Q_HEADS
    all_units = range(len(units))

    def tile_lanes(t):
        if isinstance(t, int):
            return pl.ds(t * tq, tq)
        return pl.ds(pl.multiple_of(t * tq, tq), tq)

    def load_queries(t, qslot):
        for u in all_units:
            q_ref, q_rows, half = units[u][:3]
            pieces = [zeros, zeros]
            pieces[half] = q_ref[0, q_rows, tile_lanes(t)]
            qpad_sc[qslot, u] = jnp.concatenate(pieces, axis=0)

    def block_offset(j):
        if isinstance(j, int):
            return j * tk
        return pl.multiple_of(j * tk, tk)

    def scores(j, slot, u, qslot):
        k_ref, k_lanes = units[u][3:5]
        kb = k_ref[0, pl.ds(block_offset(j), tk), k_lanes]
        s = jnp.dot(kb, qpad_sc[qslot, u], preferred_element_type=F32)
        s_sc[slot, u] = s
        bm_sc[slot, u] = jnp.max(s, axis=0, keepdims=True)

    def finish(j, slot, u, first=False):
        v_ref, v_rows, dv = units[u][5:]
        vext = jnp.concatenate([v_ref[0, v_rows, pl.ds(block_offset(j), tk)], ones], axis=0)
        rows = dv + ONES_ROWS
        if first:
            m_new = bm_sc[slot, u]
        else:
            m_prev = m_sc[u]
            m_new = jnp.maximum(m_prev, bm_sc[slot, u])
        p = jnp.exp2(s_sc[slot, u] - m_new).astype(BF16)
        pv = jnp.dot(vext, p, preferred_element_type=F32)
        if first:
            acc_sc[u, :rows] = pv
        else:
            acc_sc[u, :rows] = jnp.exp2(m_prev - m_new) * acc_sc[u, :rows] + pv
        m_sc[u] = m_new

    assert unroll % 2 == 0
    last = n_kv - 1
    assert n_tiles == 1 or last % 2 == 1, "next tile's block 0 reuses slot 0"

    def advance(j, slot, qslot, first=False):
        for u in all_units:
            scores(j + 1, 1 - slot, u, qslot)
            finish(j, slot, u, first)

    def first_block(qslot):
        if n_kv > 1:
            advance(0, 0, qslot, first=True)

    def middle_blocks(qslot):
        def body(i, carry):
            for k in range(unroll):
                advance(1 + unroll * i + k, (1 + k) % 2, qslot)
            return carry

        n_iter = max(n_kv - 2, 0) // unroll
        if n_iter > 0:
            lax.fori_loop(0, n_iter, body, 0)
        for j in range(1 + unroll * n_iter, last):
            advance(j, j % 2, qslot)

    lv = lamv_ref[...]
    lam = (jnp.exp(jnp.sum(lv[0:1] * lv[1:2], axis=-1, keepdims=True))
           - jnp.exp(jnp.sum(lv[2:3] * lv[3:4], axis=-1, keepdims=True)) + LAM_INIT)
    gain = gs_ref[...] * (1.0 - LAM_INIT)

    def result(u):
        dv = units[u][7]
        return acc_sc[u, :dv] * (1.0 / acc_sc[u, dv:dv + 1])

    def store_tile(t):
        for hd in range(A_Q_HEADS):
            oa_ref[0, hd * HEAD_DIM:(hd + 1) * HEAD_DIM, tile_lanes(t)] = result(hd).astype(BF16)
        for hd in range(B_HEADS):
            base = hd * B_V_DIM
            u = first_b_unit + 2 * hd
            d = result(u) - lam * result(u + 1)
            ob_ref[0, base:base + B_V_DIM, tile_lanes(t)] = (_rms(d, 0) * gain).astype(BF16)

    load_queries(0, 0)
    for u in all_units:
        scores(0, 0, u, 0)
    first_block(0)

    def tile_body(t, carry):
        qslot = jnp.bitwise_and(t, 1)
        middle_blocks(qslot)
        load_queries(t + 1, 1 - qslot)
        for u in all_units:
            scores(0, 0, u, 1 - qslot)
            finish(last, last % 2, u)
        store_tile(t)
        first_block(1 - qslot)
        return carry

    if n_tiles > 1:
        lax.fori_loop(0, n_tiles - 1, tile_body, 0)
    middle_blocks((n_tiles - 1) % 2)
    for u in all_units:
        finish(last, last % 2, u, first=(n_kv == 1))
    store_tile(n_tiles - 1)


def _attention(qa, qb, ka, va, kb, vb, lamv, gs, tq, tk, n_tiles):
    b, _, s = qa.shape
    t = ka.shape[1]
    qspec = pl.BlockSpec((1, A_Q_COLS, n_tiles * tq), lambda bi, i: (bi, 0, i))
    whole = lambda a: pl.BlockSpec((1,) + a.shape[1:], lambda bi, i: (bi, 0, 0))
    full = lambda a: pl.BlockSpec(a.shape, lambda bi, i: (0,) * a.ndim)
    return pl.pallas_call(
        functools.partial(_attn_kernel, tq=tq, tk=tk, n_kv=t // tk, unroll=ATTN_UNROLL,
                          n_tiles=n_tiles),
        grid=(b, s // (n_tiles * tq)),
        in_specs=[qspec, qspec, whole(ka), whole(va), whole(kb), whole(vb), full(lamv), full(gs)],
        out_specs=[qspec, qspec],
        out_shape=[jax.ShapeDtypeStruct((b, A_Q_COLS, s), BF16),
                   jax.ShapeDtypeStruct((b, B_V_COLS, s), BF16)],
        scratch_shapes=[
            pltpu.VMEM((2, ATTN_UNITS, 2 * HEAD_DIM, tq), BF16),
            pltpu.VMEM((ATTN_UNITS, 1, tq), F32),
            pltpu.VMEM((ATTN_UNITS, B_V_DIM + ONES_ROWS, tq), F32),
            pltpu.VMEM((2, ATTN_UNITS, tk, tq), F32),
            pltpu.VMEM((2, ATTN_UNITS, 1, tq), F32),
        ],
        compiler_params=pltpu.CompilerParams(
            dimension_semantics=("arbitrary", "arbitrary"), vmem_limit_bytes=VMEM_LIMIT),
        name="attention_t%d" % t,
    )(qa, qb, ka, va, kb, vb, lamv, gs)


def _out_kernel(x_ref, mods_ref, g1_ref, g2_ref, gf_ref, oa_ref, ob_ref, wg_ref, bg_ref,
                wa_ref, wb_ref, wo_ref, w1_ref, w2_ref, y_ref):
    x = x_ref[0]
    mods = mods_ref[0]
    h = _modulated_norm(x, g1_ref[...], mods[1:2], mods[0:1])
    gates = jax.nn.sigmoid(
        jnp.dot(h.astype(BF16), wg_ref[...], preferred_element_type=F32) + bg_ref[...])
    tn = (((0,), (0,)), ((), ()))
    br_a = lax.dot_general(oa_ref[0], wa_ref[...], tn, preferred_element_type=F32)
    br_b = lax.dot_general(ob_ref[0], wb_ref[...], tn, preferred_element_type=F32)
    merged = gates[:, :D_MODEL] * br_a + gates[:, D_MODEL:] * br_b
    x = x + mods[2:3] * jnp.dot(merged.astype(BF16), wo_ref[...], preferred_element_type=F32)
    h2 = _modulated_norm(x, g2_ref[...], mods[4:5], mods[3:4])
    f = jnp.dot(h2.astype(BF16), w1_ref[...], preferred_element_type=F32)
    f = jnp.square(jnp.maximum(f, 0.0))
    x = x + mods[5:6] * jnp.dot(f.astype(BF16), w2_ref[...], preferred_element_type=F32)
    y_ref[0] = _rms(x, -1) * gf_ref[...]


def _output(x, mods, mod_row, g1, g2, gf, oa, ob, wg, bg, wa, wb, wo, w1, w2, tm):
    b, s, _ = x.shape
    full = lambda a: pl.BlockSpec(a.shape, lambda bi, i: (0,) * a.ndim)
    xspec = pl.BlockSpec((1, tm, D_MODEL), lambda bi, i: (bi, i, 0))
    ospec = pl.BlockSpec((1, A_Q_COLS, tm), lambda bi, i: (bi, 0, i))
    return pl.pallas_call(
        _out_kernel,
        grid=(b, s // tm),
        in_specs=[xspec,
                  pl.BlockSpec((1, N_MOD, D_MODEL), lambda bi, i: (mod_row(bi), 0, 0)),
                  full(g1), full(g2), full(gf), ospec, ospec,
                  full(wg), full(bg), full(wa), full(wb), full(wo), full(w1), full(w2)],
        out_specs=xspec,
        out_shape=jax.ShapeDtypeStruct(x.shape, F32),
        compiler_params=pltpu.CompilerParams(
            dimension_semantics=("arbitrary", "arbitrary"), vmem_limit_bytes=VMEM_LIMIT),
        name="output_s%d" % s,
    )(x, mods, g1, g2, gf, oa, ob, wg, bg, wa, wb, wo, w1, w2)


def _rope_tables_t(n_tokens):
    pos = jnp.arange(n_tokens, dtype=jnp.int32)
    row = (pos // GRID_W).astype(F32)
    col = (pos % GRID_W).astype(F32)
    n_freq = HEAD_DIM // 4
    freqs = ROPE_BASE ** (-jnp.arange(n_freq, dtype=F32) / n_freq)
    ang = jnp.concatenate([freqs[:, None] * row[None, :], freqs[:, None] * col[None, :]], axis=0)
    return jnp.cos(ang), jnp.sin(ang)


def kernel(x_prompt, x_sample, cache_a_k, cache_a_v, cache_b_k, cache_b_v, c, c_ctx, w_mod, b_mod, norm1_g, w_in, a_q_norm_g, a_k_norm_g, lam_q1, lam_k1, lam_q2, lam_k2, b_subln_g, w_gate, b_gate, w_br_a, w_br_b, w_out, norm2_g, w_fc1, w_fc2, final_norm_g):
    nb, seq, _ = x_prompt.shape
    db, dseq, _ = x_sample.shape
    past = cache_a_k.shape[2]
    l = 0

    cond = jnp.concatenate([c, c_ctx[None, :]], axis=0)
    ctx_row = db
    rows = -(-cond.shape[0] // 8) * 8
    cond = jnp.pad(cond, ((0, rows - cond.shape[0]), (0, 0)))
    mods = _modulation(cond, w_mod[l], b_mod[l]).reshape(rows, N_MOD, D_MODEL)

    g1 = norm1_g[l].reshape(1, D_MODEL)
    g2 = norm2_g[l].reshape(1, D_MODEL)
    gf = final_norm_g.reshape(1, D_MODEL)
    gq = a_q_norm_g[l].reshape(HEAD_DIM, 1)
    gk = a_k_norm_g[l].reshape(HEAD_DIM, 1)
    gs = b_subln_g[l].reshape(B_V_DIM, 1)
    lamv = jnp.stack([lam_q1[l], lam_k1[l], lam_q2[l], lam_k2[l]], axis=0)
    w_in_t = w_in[l].T.astype(BF16)
    wg = w_gate[l].astype(BF16)
    bg = b_gate[l].reshape(1, -1)
    wa = w_br_a[l].astype(BF16)
    wb = w_br_b[l].astype(BF16)
    wo = w_out[l].astype(BF16)
    w1 = w_fc1[l].astype(BF16)
    w2 = w_fc2[l].astype(BF16)

    ctx_mod = lambda bi: ctx_row
    lat_mod = lambda bi: bi

    tm_c = min(seq, 512)
    qa, ka, va, qb, kb, vb, new_ak, new_av, new_bk, new_bv = _projection(
        x_prompt, mods, ctx_mod, g1, w_in_t, gq, gk, None, True, tm_c)
    oa, ob = _attention(qa, qb, ka, va, kb, vb, lamv, gs, tq=min(seq, 256), tk=min(seq, 256),
                        n_tiles=1)
    y_prompt = _output(x_prompt, mods, ctx_mod, g1, g2, gf, oa, ob, wg, bg, wa, wb, wo, w1, w2, tm_c)

    rope = _rope_tables_t(dseq)
    cached = (cache_a_k[:, l].reshape(db, past, A_KV_COLS), cache_a_v[:, l].reshape(db, past, A_KV_COLS),
              cache_b_k[:, l].reshape(db, past, B_QK_COLS), cache_b_v[:, l].reshape(db, past, B_V_COLS))
    qa, ka, va, qb, kb, vb = _projection(
        x_sample, mods, lat_mod, g1, w_in_t, gq, gk, rope, False, past, past=cached)
    oa, ob = _attention(qa, qb, ka, va, kb, vb, lamv, gs, tq=256, tk=256, n_tiles=4)
    y_sample = _output(x_sample, mods, lat_mod, g1, g2, gf, oa, ob, wg, bg, wa, wb, wo, w1, w2, 512)

    return (y_prompt, y_sample,
            new_ak.reshape(nb, 1, seq, A_KV_HEADS, HEAD_DIM),
            new_av.reshape(nb, 1, seq, A_KV_HEADS, HEAD_DIM),
            new_bk.reshape(nb, 1, seq, B_HEADS, 2, HEAD_DIM),
            new_bv.reshape(nb, 1, seq, B_HEADS, B_V_DIM))
```

```python
import functools

import jax
import jax.numpy as jnp
from jax import lax
from jax.experimental import pallas as pl
from jax.experimental.pallas import tpu as pltpu

F32 = jnp.float32
BF16 = jnp.bfloat16

D_MODEL = 1024
HEAD_DIM = 64
HALF = HEAD_DIM // 2
GRID_W = 64
A_Q_HEADS = 8
A_KV_HEADS = 2
A_GROUP = A_Q_HEADS // A_KV_HEADS
B_HEADS = 4
B_V_DIM = 2 * HEAD_DIM
A_Q_COLS = A_Q_HEADS * HEAD_DIM
A_KV_COLS = A_KV_HEADS * HEAD_DIM
B_QK_COLS = B_HEADS * 2 * HEAD_DIM
B_V_COLS = B_HEADS * B_V_DIM
OFF_AQ = 0
OFF_AK = OFF_AQ + A_Q_COLS
OFF_AV = OFF_AK + A_KV_COLS
OFF_BQ = OFF_AV + A_KV_COLS
OFF_BK = OFF_BQ + B_QK_COLS
OFF_BV = OFF_BK + B_QK_COLS
IN_COLS = OFF_BV + B_V_COLS
D_FF = 4 * D_MODEL
N_MOD = 6
ROPE_BASE = 10000.0
EPS = 1e-6
LAM_INIT = 0.2
LOG2E = 1.4426950408889634
Q_PRESCALE = HEAD_DIM ** -0.5 * LOG2E

V7X_VMEM_BYTES = 64 * 1024 * 1024
VMEM_LIMIT = 56 * 1024 * 1024


def _rms(x, axis):
    return x * lax.rsqrt(jnp.mean(x * x, axis=axis, keepdims=True) + EPS)


def _modulated_norm(x, g, scale, shift):
    return (_rms(x, -1) * g) * (1.0 + scale) + shift


def _mod_kernel(cond_ref, w_ref, b_ref, o_ref):
    c = cond_ref[...]
    s = c * jax.nn.sigmoid(c)
    o_ref[...] = jnp.dot(s.astype(BF16), w_ref[...].astype(BF16),
                         preferred_element_type=F32) + b_ref[...]


def _modulation(cond, w_mod, b_mod):
    rows = cond.shape[0]
    n = w_mod.shape[1]
    tn = 1536
    return pl.pallas_call(
        _mod_kernel,
        grid=(n // tn,),
        in_specs=[
            pl.BlockSpec((rows, D_MODEL), lambda j: (0, 0)),
            pl.BlockSpec((D_MODEL, tn), lambda j: (0, j)),
            pl.BlockSpec((1, tn), lambda j: (0, j)),
        ],
        out_specs=pl.BlockSpec((rows, tn), lambda j: (0, j)),
        out_shape=jax.ShapeDtypeStruct((rows, n), F32),
        compiler_params=pltpu.CompilerParams(
            dimension_semantics=("arbitrary",), vmem_limit_bytes=VMEM_LIMIT),
        name="modulation",
    )(cond, w_mod, b_mod.reshape(1, n))


PROJ_MIN_SUBTILE = 256
PROJ_ROW_CHUNKS = ((OFF_AQ, OFF_AK), (OFF_AK, OFF_BK), (OFF_BK, OFF_BV), (OFF_BV, IN_COLS))


def _rope_t(x, cos, sin):
    x1, x2 = x[:HALF], x[HALF:]
    return jnp.concatenate([x1 * cos - x2 * sin, x1 * sin + x2 * cos], axis=0)


def _proj_kernel(*refs, use_rope, emit_cache, with_past):
    it = iter(refs)
    x_ref, mods_ref, g1_ref, w_ref, gq_ref, gk_ref = (next(it) for _ in range(6))
    cos_ref = sin_ref = None
    if use_rope:
        cos_ref, sin_ref = next(it), next(it)
    if with_past:
        pak_ref, pav_ref, pbk_ref, pbv_ref = (next(it) for _ in range(4))
    qa_ref, ka_ref, va_ref, qb_ref, kb_ref, vb_ref = (next(it) for _ in range(6))
    if emit_cache:
        cak_ref, cav_ref, cbk_ref, cbv_ref = (next(it) for _ in range(4))

    tm = x_ref.shape[1]
    n_sub = 2 if tm % (2 * PROJ_MIN_SUBTILE) == 0 else 1
    ts = tm // n_sub

    def project():
        mods = mods_ref[0]
        g1 = g1_ref[...]
        nt = (((1,), (1,)), ((), ()))
        t = []
        for s in range(n_sub):
            x = x_ref[0, s * ts:(s + 1) * ts, :]
            h = _modulated_norm(x, g1, mods[1:2], mods[0:1]).astype(BF16)
            t.append([lax.dot_general(w_ref[r0:r1, :], h, nt, preferred_element_type=F32)
                      for r0, r1 in PROJ_ROW_CHUNKS])
        gq, gk = gq_ref[...], gk_ref[...]

        for s in range(n_sub):
            cols = slice(s * ts, (s + 1) * ts)
            if use_rope:
                cos, sin = cos_ref[:, cols], sin_ref[:, cols]

            def rows(r0, r1):
                for c, (c0, c1) in enumerate(PROJ_ROW_CHUNKS):
                    if c0 <= r0 and r1 <= c1:
                        return t[s][c][r0 - c0:r1 - c0]
                raise ValueError("row range crosses a projection chunk")

            def head(off, gain=None):
                blk = rows(off, off + HEAD_DIM)
                if gain is not None:
                    blk = _rms(blk, 0) * gain
                if use_rope:
                    blk = _rope_t(blk, cos, sin)
                return blk

            for i in range(A_Q_HEADS):
                off = OFF_AQ + i * HEAD_DIM
                qa_ref[0, i * HEAD_DIM:(i + 1) * HEAD_DIM, cols] = (
                    head(off, gq) * Q_PRESCALE).astype(BF16)
            ak_t = jnp.concatenate(
                [head(OFF_AK + i * HEAD_DIM, gk) for i in range(A_KV_HEADS)], axis=0)
            ak = ak_t.T
            ka_ref[0, cols, :] = ak.astype(BF16)
            av_t = rows(OFF_AV, OFF_AV + A_KV_COLS)
            va_ref[0, :, cols] = av_t.astype(BF16)
            for i in range(2 * B_HEADS):
                off = OFF_BQ + i * HEAD_DIM
                qb_ref[0, i * HEAD_DIM:(i + 1) * HEAD_DIM, cols] = (
                    head(off) * Q_PRESCALE).astype(BF16)
            bk_t = jnp.concatenate(
                [head(OFF_BK + i * HEAD_DIM) for i in range(2 * B_HEADS)], axis=0)
            bk = bk_t.T
            kb_ref[0, cols, :] = bk.astype(BF16)
            bv_t = rows(OFF_BV, OFF_BV + B_V_COLS)
            vb_ref[0, :, cols] = bv_t.astype(BF16)
            if emit_cache:
                cak_ref[0, cols, :] = ak
                cbk_ref[0, cols, :] = bk
                cav_ref[0, cols, :] = av_t.T
                cbv_ref[0, cols, :] = bv_t.T

    if not with_past:
        project()
        return

    step = pl.program_id(1)

    @pl.when(step == 0)
    def _():
        ka_ref[0] = pak_ref[0].astype(BF16)
        kb_ref[0] = pbk_ref[0].astype(BF16)
        va_ref[0] = pav_ref[0].T.astype(BF16)
        vb_ref[0] = pbv_ref[0].T.astype(BF16)

    pl.when(step > 0)(project)


def _projection(x, mods, mod_row, g1, w_in_t, gq, gk, rope, emit_cache, tm, past=None):
    b, s, _ = x.shape
    use_rope = rope is not None
    with_past = past is not None
    lead = 1 if with_past else 0
    if with_past:
        assert all(p.shape[1] == tm for p in past), "cached context must fill exactly one key block"
    tok = lambda i: jnp.maximum(i - lead, 0)
    full = lambda shape: pl.BlockSpec(shape, lambda bi, i: (0,) * len(shape))
    in_specs = [
        pl.BlockSpec((1, tm, D_MODEL), lambda bi, i: (bi, tok(i), 0)),
        pl.BlockSpec((1, N_MOD, D_MODEL), lambda bi, i: (mod_row(bi), 0, 0)),
        full((1, D_MODEL)),
        full((IN_COLS, D_MODEL)),
        full((HEAD_DIM, 1)),
        full((HEAD_DIM, 1)),
    ]
    args = [x, mods, g1, w_in_t, gq, gk]
    if use_rope:
        in_specs += [pl.BlockSpec((HALF, tm), lambda bi, i: (0, tok(i)))] * 2
        args += list(rope)
    if with_past:
        in_specs += [pl.BlockSpec((1,) + p.shape[1:], lambda bi, i: (bi, 0, 0)) for p in past]
        args += list(past)
    keys = s + lead * tm
    q_spec = lambda rows: pl.BlockSpec((1, rows, tm), lambda bi, i: (bi, 0, tok(i)))
    fm = lambda rows: pl.BlockSpec((1, rows, tm), lambda bi, i: (bi, 0, i))
    tk = lambda cols: pl.BlockSpec((1, tm, cols), lambda bi, i: (bi, i, 0))
    out_specs = [q_spec(A_Q_COLS), tk(A_KV_COLS), fm(A_KV_COLS),
                 q_spec(B_QK_COLS), tk(B_QK_COLS), fm(B_V_COLS)]
    out_shape = [
        jax.ShapeDtypeStruct((b, A_Q_COLS, s), BF16),
        jax.ShapeDtypeStruct((b, keys, A_KV_COLS), BF16),
        jax.ShapeDtypeStruct((b, A_KV_COLS, keys), BF16),
        jax.ShapeDtypeStruct((b, B_QK_COLS, s), BF16),
        jax.ShapeDtypeStruct((b, keys, B_QK_COLS), BF16),
        jax.ShapeDtypeStruct((b, B_V_COLS, keys), BF16),
    ]
    if emit_cache:
        assert not with_past
        out_specs += [tk(A_KV_COLS), tk(A_KV_COLS), tk(B_QK_COLS), tk(B_V_COLS)]
        out_shape += [
            jax.ShapeDtypeStruct((b, s, A_KV_COLS), F32),
            jax.ShapeDtypeStruct((b, s, A_KV_COLS), F32),
            jax.ShapeDtypeStruct((b, s, B_QK_COLS), F32),
            jax.ShapeDtypeStruct((b, s, B_V_COLS), F32),
        ]
    return pl.pallas_call(
        functools.partial(_proj_kernel, use_rope=use_rope, emit_cache=emit_cache,
                          with_past=with_past),
        grid=(b, s // tm + lead),
        in_specs=in_specs,
        out_specs=out_specs,
        out_shape=out_shape,
        compiler_params=pltpu.CompilerParams(
            dimension_semantics=("arbitrary", "arbitrary"), vmem_limit_bytes=VMEM_LIMIT),
        name="projection_rope" if use_rope else "projection_ctx",
    )(*args)


ONES_ROWS = 16
ATTN_UNITS = A_Q_HEADS + 2 * B_HEADS
ATTN_UNROLL = 8


def _attn_kernel(qa_ref, qb_ref, ka_ref, va_ref, kb_ref, vb_ref, lamv_ref, gs_ref,
                 oa_ref, ob_ref, qpad_sc, m_sc, acc_sc, s_sc, bm_sc, *, tq, tk, n_kv, unroll, n_tiles):
    zeros = jnp.zeros((HEAD_DIM, tq), BF16)
    ones = jnp.ones((ONES_ROWS, tk), BF16)

    units = []
    for hd in range(A_Q_HEADS):
        g = hd // A_GROUP
        units.append((qa_ref, slice(hd * HEAD_DIM, (hd + 1) * HEAD_DIM), g,
                      ka_ref, slice(None), va_ref, slice(g * HEAD_DIM, (g + 1) * HEAD_DIM), HEAD_DIM))
    for hd in range(B_HEADS):
        base = hd * B_V_DIM
        for c in range(2):
            units.append((qb_ref, slice(base + c * HEAD_DIM, base + (c + 1) * HEAD_DIM), c,
                          kb_ref, slice(base, base + B_V_DIM), vb_ref, slice(base, base + B_V_DIM),
                          B_V_DIM))
    first_b_unit = A_Q_HEADS
    all_units = range(len(units))

    def tile_lanes(t):
        if isinstance(t, int):
            return pl.ds(t * tq, tq)
        return pl.ds(pl.multiple_of(t * tq, tq), tq)

    def load_queries(t, qslot):
        for u in all_units:
            q_ref, q_rows, half = units[u][:3]
            pieces = [zeros, zeros]
            pieces[half] = q_ref[0, q_rows, tile_lanes(t)]
            qpad_sc[qslot, u] = jnp.concatenate(pieces, axis=0)

    def block_offset(j):
        if isinstance(j, int):
            return j * tk
        return pl.multiple_of(j * tk, tk)

    def scores(j, slot, u, qslot):
        k_ref, k_lanes = units[u][3:5]
        kb = k_ref[0, pl.ds(block_offset(j), tk), k_lanes]
        s = jnp.dot(kb, qpad_sc[qslot, u], preferred_element_type=F32)
        s_sc[slot, u] = s
        bm_sc[slot, u] = jnp.max(s, axis=0, keepdims=True)

    def finish(j, slot, u, first=False):
        v_ref, v_rows, dv = units[u][5:]
        vext = jnp.concatenate([v_ref[0, v_rows, pl.ds(block_offset(j), tk)], ones], axis=0)
        rows = dv + ONES_ROWS
        if first:
            m_new = bm_sc[slot, u]
        else:
            m_prev = m_sc[u]
            m_new = jnp.maximum(m_prev, bm_sc[slot, u])
        p = jnp.exp2(s_sc[slot, u] - m_new).astype(BF16)
        pv = jnp.dot(vext, p, preferred_element_type=F32)
        if first:
            acc_sc[u, :rows] = pv
        else:
            acc_sc[u, :rows] = jnp.exp2(m_prev - m_new) * acc_sc[u, :rows] + pv
        m_sc[u] = m_new

    assert unroll % 2 == 0
    last = n_kv - 1
    assert n_tiles == 1 or last % 2 == 1, "next tile's block 0 reuses slot 0"

    def advance(j, slot, qslot, first=False):
        for u in all_units:
            scores(j + 1, 1 - slot, u, qslot)
            finish(j, slot, u, first)

    def first_block(qslot):
        if n_kv > 1:
            advance(0, 0, qslot, first=True)

    def middle_blocks(qslot):
        def body(i, carry):
            for k in range(unroll):
                advance(1 + unroll * i + k, (1 + k) % 2, qslot)
            return carry

        n_iter = max(n_kv - 2, 0) // unroll
        if n_iter > 0:
            lax.fori_loop(0, n_iter, body, 0)
        for j in range(1 + unroll * n_iter, last):
            advance(j, j % 2, qslot)

    lv = lamv_ref[...]
    lam = (jnp.exp(jnp.sum(lv[0:1] * lv[1:2], axis=-1, keepdims=True))
           - jnp.exp(jnp.sum(lv[2:3] * lv[3:4], axis=-1, keepdims=True)) + LAM_INIT)
    gain = gs_ref[...] * (1.0 - LAM_INIT)

    def result(u):
        dv = units[u][7]
        return acc_sc[u, :dv] * (1.0 / acc_sc[u, dv:dv + 1])

    def store_tile(t):
        for hd in range(A_Q_HEADS):
            oa_ref[0, hd * HEAD_DIM:(hd + 1) * HEAD_DIM, tile_lanes(t)] = result(hd).astype(BF16)
        for hd in range(B_HEADS):
            base = hd * B_V_DIM
            u = first_b_unit + 2 * hd
            d = result(u) - lam * result(u + 1)
            ob_ref[0, base:base + B_V_DIM, tile_lanes(t)] = (_rms(d, 0) * gain).astype(BF16)

    load_queries(0, 0)
    for u in all_units:
        scores(0, 0, u, 0)
    first_block(0)

    def tile_body(t, carry):
        qslot = jnp.bitwise_and(t, 1)
        middle_blocks(qslot)
        load_queries(t + 1, 1 - qslot)
        for u in all_units:
            scores(0, 0, u, 1 - qslot)
            finish(last, last % 2, u)
        store_tile(t)
        first_block(1 - qslot)
        return carry

    if n_tiles > 1:
        lax.fori_loop(0, n_tiles - 1, tile_body, 0)
    middle_blocks((n_tiles - 1) % 2)
    for u in all_units:
        finish(last, last % 2, u, first=(n_kv == 1))
    store_tile(n_tiles - 1)


def _attention(qa, qb, ka, va, kb, vb, lamv, gs, tq, tk, n_tiles):
    b, _, s = qa.shape
    t = ka.shape[1]
    qspec = pl.BlockSpec((1, A_Q_COLS, n_tiles * tq), lambda bi, i: (bi, 0, i))
    whole = lambda a: pl.BlockSpec((1,) + a.shape[1:], lambda bi, i: (bi, 0, 0))
    full = lambda a: pl.BlockSpec(a.shape, lambda bi, i: (0,) * a.ndim)
    return pl.pallas_call(
        functools.partial(_attn_kernel, tq=tq, tk=tk, n_kv=t // tk, unroll=ATTN_UNROLL,
                          n_tiles=n_tiles),
        grid=(b, s // (n_tiles * tq)),
        in_specs=[qspec, qspec, whole(ka), whole(va), whole(kb), whole(vb), full(lamv), full(gs)],
        out_specs=[qspec, qspec],
        out_shape=[jax.ShapeDtypeStruct((b, A_Q_COLS, s), BF16),
                   jax.ShapeDtypeStruct((b, B_V_COLS, s), BF16)],
        scratch_shapes=[
            pltpu.VMEM((2, ATTN_UNITS, 2 * HEAD_DIM, tq), BF16),
            pltpu.VMEM((ATTN_UNITS, 1, tq), F32),
            pltpu.VMEM((ATTN_UNITS, B_V_DIM + ONES_ROWS, tq), F32),
            pltpu.VMEM((2, ATTN_UNITS, tk, tq), F32),
            pltpu.VMEM((2, ATTN_UNITS, 1, tq), F32),
        ],
        compiler_params=pltpu.CompilerParams(
            dimension_semantics=("arbitrary", "arbitrary"), vmem_limit_bytes=VMEM_LIMIT),
        name="attention_t%d" % t,
    )(qa, qb, ka, va, kb, vb, lamv, gs)


def _out_kernel(x_ref, mods_ref, g1_ref, g2_ref, gf_ref, oa_ref, ob_ref, wg_ref, bg_ref,
                wa_ref, wb_ref, wo_ref, w1_ref, w2_ref, y_ref):
    x = x_ref[0]
    mods = mods_ref[0]
    h = _modulated_norm(x, g1_ref[...], mods[1:2], mods[0:1])
    gates = jax.nn.sigmoid(
        jnp.dot(h.astype(BF16), wg_ref[...], preferred_element_type=F32) + bg_ref[...])
    tn = (((0,), (0,)), ((), ()))
    br_a = lax.dot_general(oa_ref[0], wa_ref[...], tn, preferred_element_type=F32)
    br_b = lax.dot_general(ob_ref[0], wb_ref[...], tn, preferred_element_type=F32)
    merged = gates[:, :D_MODEL] * br_a + gates[:, D_MODEL:] * br_b
    x = x + mods[2:3] * jnp.dot(merged.astype(BF16), wo_ref[...], preferred_element_type=F32)
    h2 = _modulated_norm(x, g2_ref[...], mods[4:5], mods[3:4])
    f = jnp.dot(h2.astype(BF16), w1_ref[...], preferred_element_type=F32)
    f = jnp.square(jnp.maximum(f, 0.0))
    x = x + mods[5:6] * jnp.dot(f.astype(BF16), w2_ref[...], preferred_element_type=F32)
    y_ref[0] = _rms(x, -1) * gf_ref[...]


def _output(x, mods, mod_row, g1, g2, gf, oa, ob, wg, bg, wa, wb, wo, w1, w2, tm):
    b, s, _ = x.shape
    full = lambda a: pl.BlockSpec(a.shape, lambda bi, i: (0,) * a.ndim)
    xspec = pl.BlockSpec((1, tm, D_MODEL), lambda bi, i: (bi, i, 0))
    ospec = pl.BlockSpec((1, A_Q_COLS, tm), lambda bi, i: (bi, 0, i))
    return pl.pallas_call(
        _out_kernel,
        grid=(b, s // tm),
        in_specs=[xspec,
                  pl.BlockSpec((1, N_MOD, D_MODEL), lambda bi, i: (mod_row(bi), 0, 0)),
                  full(g1), full(g2), full(gf), ospec, ospec,
                  full(wg), full(bg), full(wa), full(wb), full(wo), full(w1), full(w2)],
        out_specs=xspec,
        out_shape=jax.ShapeDtypeStruct(x.shape, F32),
        compiler_params=pltpu.CompilerParams(
            dimension_semantics=("arbitrary", "arbitrary"), vmem_limit_bytes=VMEM_LIMIT),
        name="output_s%d" % s,
    )(x, mods, g1, g2, gf, oa, ob, wg, bg, wa, wb, wo, w1, w2)


def _rope_tables_t(n_tokens):
    pos = jnp.arange(n_tokens, dtype=jnp.int32)
    row = (pos // GRID_W).astype(F32)
    col = (pos % GRID_W).astype(F32)
    n_freq = HEAD_DIM // 4
    freqs = ROPE_BASE ** (-jnp.arange(n_freq, dtype=F32) / n_freq)
    ang = jnp.concatenate([freqs[:, None] * row[None, :], freqs[:, None] * col[None, :]], axis=0)
    return jnp.cos(ang), jnp.sin(ang)


def kernel(x_prompt, x_sample, cache_a_k, cache_a_v, cache_b_k, cache_b_v, c, c_ctx, w_mod, b_mod, norm1_g, w_in, a_q_norm_g, a_k_norm_g, lam_q1, lam_k1, lam_q2, lam_k2, b_subln_g, w_gate, b_gate, w_br_a, w_br_b, w_out, norm2_g, w_fc1, w_fc2, final_norm_g):
    nb, seq, _ = x_prompt.shape
    db, dseq, _ = x_sample.shape
    past = cache_a_k.shape[2]
    l = 0

    cond = jnp.concatenate([c, c_ctx[None, :]], axis=0)
    ctx_row = db
    rows = -(-cond.shape[0] // 8) * 8
    cond = jnp.pad(cond, ((0, rows - cond.shape[0]), (0, 0)))
    mods = _modulation(cond, w_mod[l], b_mod[l]).reshape(rows, N_MOD, D_MODEL)

    g1 = norm1_g[l].reshape(1, D_MODEL)
    g2 = norm2_g[l].reshape(1, D_MODEL)
    gf = final_norm_g.reshape(1, D_MODEL)
    gq = a_q_norm_g[l].reshape(HEAD_DIM, 1)
    gk = a_k_norm_g[l].reshape(HEAD_DIM, 1)
    gs = b_subln_g[l].reshape(B_V_DIM, 1)
    lamv = jnp.stack([lam_q1[l], lam_k1[l], lam_q2[l], lam_k2[l]], axis=0)
    w_in_t = w_in[l].T.astype(BF16)
    wg = w_gate[l].astype(BF16)
    bg = b_gate[l].reshape(1, -1)
    wa = w_br_a[l].astype(BF16)
    wb = w_br_b[l].astype(BF16)
    wo = w_out[l].astype(BF16)
    w1 = w_fc1[l].astype(BF16)
    w2 = w_fc2[l].astype(BF16)

    ctx_mod = lambda bi: ctx_row
    lat_mod = lambda bi: bi

    tm_c = min(seq, 512)
    qa, ka, va, qb, kb, vb, new_ak, new_av, new_bk, new_bv = _projection(
        x_prompt, mods, ctx_mod, g1, w_in_t, gq, gk, None, True, tm_c)
    oa, ob = _attention(qa, qb, ka, va, kb, vb, lamv, gs, tq=min(seq, 256), tk=min(seq, 256),
                        n_tiles=1)
    y_prompt = _output(x_prompt, mods, ctx_mod, g1, g2, gf, oa, ob, wg, bg, wa, wb, wo, w1, w2, tm_c)

    rope = _rope_tables_t(dseq)
    cached = (cache_a_k[:, l].reshape(db, past, A_KV_COLS), cache_a_v[:, l].reshape(db, past, A_KV_COLS),
              cache_b_k[:, l].reshape(db, past, B_QK_COLS), cache_b_v[:, l].reshape(db, past, B_V_COLS))
    qa, ka, va, qb, kb, vb = _projection(
        x_sample, mods, lat_mod, g1, w_in_t, gq, gk, rope, False, past, past=cached)
    oa, ob = _attention(qa, qb, ka, va, kb, vb, lamv, gs, tq=256, tk=256, n_tiles=4)
    y_sample = _output(x_sample, mods, lat_mod, g1, g2, gf, oa, ob, wg, bg, wa, wb, wo, w1, w2, 512)

    return (y_prompt, y_sample,
            new_ak.reshape(nb, 1, seq, A_KV_HEADS, HEAD_DIM),
            new_av.reshape(nb, 1, seq, A_KV_HEADS, HEAD_DIM),
            new_bk.reshape(nb, 1, seq, B_HEADS, 2, HEAD_DIM),
            new_bv.reshape(nb, 1, seq, B_HEADS, B_V_DIM))
```

```python
import functools

import jax
import jax.numpy as jnp
from jax import lax
from jax.experimental import pallas as pl
from jax.experimental.pallas import tpu as pltpu

F32 = jnp.float32
BF16 = jnp.bfloat16

D_MODEL = 1024
HEAD_DIM = 64
HALF = HEAD_DIM // 2
GRID_W = 64
A_Q_HEADS = 8
A_KV_HEADS = 2
A_GROUP = A_Q_HEADS // A_KV_HEADS
B_HEADS = 4
B_V_DIM = 2 * HEAD_DIM
A_Q_COLS = A_Q_HEADS * HEAD_DIM
A_KV_COLS = A_KV_HEADS * HEAD_DIM
B_QK_COLS = B_HEADS * 2 * HEAD_DIM
B_V_COLS = B_HEADS * B_V_DIM
OFF_AQ = 0
OFF_AK = OFF_AQ + A_Q_COLS
OFF_AV = OFF_AK + A_KV_COLS
OFF_BQ = OFF_AV + A_KV_COLS
OFF_BK = OFF_BQ + B_QK_COLS
OFF_BV = OFF_BK + B_QK_COLS
IN_COLS = OFF_BV + B_V_COLS
D_FF = 4 * D_MODEL
N_MOD = 6
ROPE_BASE = 10000.0
EPS = 1e-6
LAM_INIT = 0.2
LOG2E = 1.4426950408889634
Q_PRESCALE = HEAD_DIM ** -0.5 * LOG2E

V7X_VMEM_BYTES = 64 * 1024 * 1024
VMEM_LIMIT = 56 * 1024 * 1024


def _rms(x, axis):
    return x * lax.rsqrt(jnp.mean(x * x, axis=axis, keepdims=True) + EPS)


def _modulated_norm(x, g, scale, shift):
    return (_rms(x, -1) * g) * (1.0 + scale) + shift


def _mod_kernel(cond_ref, w_ref, b_ref, o_ref):
    c = cond_ref[...]
    s = c * jax.nn.sigmoid(c)
    o_ref[...] = jnp.dot(s.astype(BF16), w_ref[...].astype(BF16),
                         preferred_element_type=F32) + b_ref[...]


def _modulation(cond, w_mod, b_mod):
    rows = cond.shape[0]
    n = w_mod.shape[1]
    tn = 1536
    return pl.pallas_call(
        _mod_kernel,
        grid=(n // tn,),
        in_specs=[
            pl.BlockSpec((rows, D_MODEL), lambda j: (0, 0)),
            pl.BlockSpec((D_MODEL, tn), lambda j: (0, j)),
            pl.BlockSpec((1, tn), lambda j: (0, j)),
        ],
        out_specs=pl.BlockSpec((rows, tn), lambda j: (0, j)),
        out_shape=jax.ShapeDtypeStruct((rows, n), F32),
        compiler_params=pltpu.CompilerParams(
            dimension_semantics=("arbitrary",), vmem_limit_bytes=VMEM_LIMIT),
        name="modulation",
    )(cond, w_mod, b_mod.reshape(1, n))


PROJ_MIN_SUBTILE = 256
PROJ_ROW_CHUNKS = ((OFF_AQ, OFF_AK), (OFF_AK, OFF_BK), (OFF_BK, OFF_BV), (OFF_BV, IN_COLS))


def _rope_t(x, cos, sin):
    x1, x2 = x[:HALF], x[HALF:]
    return jnp.concatenate([x1 * cos - x2 * sin, x1 * sin + x2 * cos], axis=0)


def _proj_kernel(*refs, use_rope, emit_cache, with_past):
    it = iter(refs)
    x_ref, mods_ref, g1_ref, w_ref, gq_ref, gk_ref = (next(it) for _ in range(6))
    cos_ref = sin_ref = None
    if use_rope:
        cos_ref, sin_ref = next(it), next(it)
    if with_past:
        pak_ref, pav_ref, pbk_ref, pbv_ref = (next(it) for _ in range(4))
    qa_ref, ka_ref, va_ref, qb_ref, kb_ref, vb_ref = (next(it) for _ in range(6))
    if emit_cache:
        cak_ref, cav_ref, cbk_ref, cbv_ref = (next(it) for _ in range(4))

    tm = x_ref.shape[1]
    n_sub = 2 if tm % (2 * PROJ_MIN_SUBTILE) == 0 else 1
    ts = tm // n_sub

    def project():
        mods = mods_ref[0]
        g1 = g1_ref[...]
        nt = (((1,), (1,)), ((), ()))
        t = []
        for s in range(n_sub):
            x = x_ref[0, s * ts:(s + 1) * ts, :]
            h = _modulated_norm(x, g1, mods[1:2], mods[0:1]).astype(BF16)
            t.append([lax.dot_general(w_ref[r0:r1, :], h, nt, preferred_element_type=F32)
                      for r0, r1 in PROJ_ROW_CHUNKS])
        gq, gk = gq_ref[...], gk_ref[...]

        for s in range(n_sub):
            cols = slice(s * ts, (s + 1) * ts)
            if use_rope:
                cos, sin = cos_ref[:, cols], sin_ref[:, cols]

            def rows(r0, r1):
                for c, (c0, c1) in enumerate(PROJ_ROW_CHUNKS):
                    if c0 <= r0 and r1 <= c1:
                        return t[s][c][r0 - c0:r1 - c0]
                raise ValueError("row range crosses a projection chunk")

            def head(off, gain=None):
                blk = rows(off, off + HEAD_DIM)
                if gain is not None:
                    blk = _rms(blk, 0) * gain
                if use_rope:
                    blk = _rope_t(blk, cos, sin)
                return blk

            for i in range(A_Q_HEADS):
                off = OFF_AQ + i * HEAD_DIM
                qa_ref[0, i * HEAD_DIM:(i + 1) * HEAD_DIM, cols] = (
                    head(off, gq) * Q_PRESCALE).astype(BF16)
            ak_t = jnp.concatenate(
                [head(OFF_AK + i * HEAD_DIM, gk) for i in range(A_KV_HEADS)], axis=0)
            ak = ak_t.T
            ka_ref[0, cols, :] = ak.astype(BF16)
            av_t = rows(OFF_AV, OFF_AV + A_KV_COLS)
            va_ref[0, :, cols] = av_t.astype(BF16)
            for i in range(2 * B_HEADS):
                off = OFF_BQ + i * HEAD_DIM
                qb_ref[0, i * HEAD_DIM:(i + 1) * HEAD_DIM, cols] = (
                    head(off) * Q_PRESCALE).astype(BF16)
            bk_t = jnp.concatenate(
                [head(OFF_BK + i * HEAD_DIM) for i in range(2 * B_HEADS)], axis=0)
            bk = bk_t.T
            kb_ref[0, cols, :] = bk.astype(BF16)
            bv_t = rows(OFF_BV, OFF_BV + B_V_COLS)
            vb_ref[0, :, cols] = bv_t.astype(BF16)
            if emit_cache:
                cak_ref[0, cols, :] = ak
                cbk_ref[0, cols, :] = bk
                cav_ref[0, cols, :] = av_t.T
                cbv_ref[0, cols, :] = bv_t.T

    if not with_past:
        project()
        return

    step = pl.program_id(1)

    @pl.when(step == 0)
    def _():
        ka_ref[0] = pak_ref[0].astype(BF16)
        kb_ref[0] = pbk_ref[0].astype(BF16)
        va_ref[0] = pav_ref[0].T.astype(BF16)
        vb_ref[0] = pbv_ref[0].T.astype(BF16)

    pl.when(step > 0)(project)


def _projection(x, mods, mod_row, g1, w_in_t, gq, gk, rope, emit_cache, tm, past=None):
    b, s, _ = x.shape
    use_rope = rope is not None
    with_past = past is not None
    lead = 1 if with_past else 0
    if with_past:
        assert all(p.shape[1] == tm for p in past), "cached context must fill exactly one key block"
    tok = lambda i: jnp.maximum(i - lead, 0)
    full = lambda shape: pl.BlockSpec(shape, lambda bi, i: (0,) * len(shape))
    in_specs = [
        pl.BlockSpec((1, tm, D_MODEL), lambda bi, i: (bi, tok(i), 0)),
        pl.BlockSpec((1, N_MOD, D_MODEL), lambda bi, i: (mod_row(bi), 0, 0)),
        full((1, D_MODEL)),
        full((IN_COLS, D_MODEL)),
        full((HEAD_DIM, 1)),
        full((HEAD_DIM, 1)),
    ]
    args = [x, mods, g1, w_in_t, gq, gk]
    if use_rope:
        in_specs += [pl.BlockSpec((HALF, tm), lambda bi, i: (0, tok(i)))] * 2
        args += list(rope)
    if with_past:
        in_specs += [pl.BlockSpec((1,) + p.shape[1:], lambda bi, i: (bi, 0, 0)) for p in past]
        args += list(past)
    keys = s + lead * tm
    q_spec = lambda rows: pl.BlockSpec((1, rows, tm), lambda bi, i: (bi, 0, tok(i)))
    fm = lambda rows: pl.BlockSpec((1, rows, tm), lambda bi, i: (bi, 0, i))
    tk = lambda cols: pl.BlockSpec((1, tm, cols), lambda bi, i: (bi, i, 0))
    out_specs = [q_spec(A_Q_COLS), tk(A_KV_COLS), fm(A_KV_COLS),
                 q_spec(B_QK_COLS), tk(B_QK_COLS), fm(B_V_COLS)]
    out_shape = [
        jax.ShapeDtypeStruct((b, A_Q_COLS, s), BF16),
        jax.ShapeDtypeStruct((b, keys, A_KV_COLS), BF16),
        jax.ShapeDtypeStruct((b, A_KV_COLS, keys), BF16),
        jax.ShapeDtypeStruct((b, B_QK_COLS, s), BF16),
        jax.ShapeDtypeStruct((b, keys, B_QK_COLS), BF16),
        jax.ShapeDtypeStruct((b, B_V_COLS, keys), BF16),
    ]
    if emit_cache:
        assert not with_past
        out_specs += [tk(A_KV_COLS), tk(A_KV_COLS), tk(B_QK_COLS), tk(B_V_COLS)]
        out_shape += [
            jax.ShapeDtypeStruct((b, s, A_KV_COLS), F32),
            jax.ShapeDtypeStruct((b, s, A_KV_COLS), F32),
            jax.ShapeDtypeStruct((b, s, B_QK_COLS), F32),
            jax.ShapeDtypeStruct((b, s, B_V_COLS), F32),
        ]
    return pl.pallas_call(
        functools.partial(_proj_kernel, use_rope=use_rope, emit_cache=emit_cache,
                          with_past=with_past),
        grid=(b, s // tm + lead),
        in_specs=in_specs,
        out_specs=out_specs,
        out_shape=out_shape,
        compiler_params=pltpu.CompilerParams(
            dimension_semantics=("arbitrary", "arbitrary"), vmem_limit_bytes=VMEM_LIMIT),
        name="projection_rope" if use_rope else "projection_ctx",
    )(*args)


ONES_ROWS = 16
ATTN_UNITS = A_Q_HEADS + 2 * B_HEADS
ATTN_UNROLL = 8


def _attn_kernel(qa_ref, qb_ref, ka_ref, va_ref, kb_ref, vb_ref, lamv_ref, gs_ref,
                 oa_ref, ob_ref, qpad_sc, m_sc, acc_sc, s_sc, bm_sc, *, tq, tk, n_kv, unroll, n_tiles):
    zeros = jnp.zeros((HEAD_DIM, tq), BF16)
    ones = jnp.ones((ONES_ROWS, tk), BF16)

    units = []
    for hd in range(A_Q_HEADS):
        g = hd // A_GROUP
        units.append((qa_ref, slice(hd * HEAD_DIM, (hd + 1) * HEAD_DIM), g,
                      ka_ref, slice(None), va_ref, slice(g * HEAD_DIM, (g + 1) * HEAD_DIM), HEAD_DIM))
    for hd in range(B_HEADS):
        base = hd * B_V_DIM
        for c in range(2):
            units.append((qb_ref, slice(base + c * HEAD_DIM, base + (c + 1) * HEAD_DIM), c,
                          kb_ref, slice(base, base + B_V_DIM), vb_ref, slice(base, base + B_V_DIM),
                          B_V_DIM))
    first_b_unit = A_Q_HEADS
    all_units = range(len(units))

    def tile_lanes(t):
        if isinstance(t, int):
            return pl.ds(t * tq, tq)
        return pl.ds(pl.multiple_of(t * tq, tq), tq)

    def load_queries(t, qslot):
        for u in all_units:
            q_ref, q_rows, half = units[u][:3]
            pieces = [zeros, zeros]
            pieces[half] = q_ref[0, q_rows, tile_lanes(t)]
            qpad_sc[qslot, u] = jnp.concatenate(pieces, axis=0)

    def block_offset(j):
        if isinstance(j, int):
            return j * tk
        return pl.multiple_of(j * tk, tk)

    def scores(j, slot, u, qslot):
        k_ref, k_lanes = units[u][3:5]
        kb = k_ref[0, pl.ds(block_offset(j), tk), k_lanes]
        s = jnp.dot(kb, qpad_sc[qslot, u], preferred_element_type=F32)
        s_sc[slot, u] = s
        bm_sc[slot, u] = jnp.max(s, axis=0, keepdims=True)

    def finish(j, slot, u, first=False):
        v_ref, v_rows, dv = units[u][5:]
        vext = jnp.concatenate([v_ref[0, v_rows, pl.ds(block_offset(j), tk)], ones], axis=0)
        rows = dv + ONES_ROWS
        if first:
            m_new = bm_sc[slot, u]
        else:
            m_prev = m_sc[u]
            m_new = jnp.maximum(m_prev, bm_sc[slot, u])
        p = jnp.exp2(s_sc[slot, u] - m_new).astype(BF16)
        pv = jnp.dot(vext, p, preferred_element_type=F32)
        if first:
            acc_sc[u, :rows] = pv
        else:
            acc_sc[u, :rows] = jnp.exp2(m_prev - m_new) * acc_sc[u, :rows] + pv
        m_sc[u] = m_new

    assert unroll % 2 == 0
    last = n_kv - 1
    assert n_tiles == 1 or last % 2 == 1, "next tile's block 0 reuses slot 0"

    def advance(j, slot, qslot, first=False):
        for u in all_units:
            scores(j + 1, 1 - slot, u, qslot)
            finish(j, slot, u, first)

    def first_block(qslot):
        if n_kv > 1:
            advance(0, 0, qslot, first=True)

    def middle_blocks(qslot):
        def body(i, carry):
            for k in range(unroll):
                advance(1 + unroll * i + k, (1 + k) % 2, qslot)
            return carry

        n_iter = max(n_kv - 2, 0) // unroll
        if n_iter > 0:
            lax.fori_loop(0, n_iter, body, 0)
        for j in range(1 + unroll * n_iter, last):
            advance(j, j % 2, qslot)

    lv = lamv_ref[...]
    lam = (jnp.exp(jnp.sum(lv[0:1] * lv[1:2], axis=-1, keepdims=True))
           - jnp.exp(jnp.sum(lv[2:3] * lv[3:4], axis=-1, keepdims=True)) + LAM_INIT)
    gain = gs_ref[...] * (1.0 - LAM_INIT)

    def result(u):
        dv = units[u][7]
        return acc_sc[u, :dv] * (1.0 / acc_sc[u, dv:dv + 1])

    def store_tile(t):
        for hd in range(A_Q_HEADS):
            oa_ref[0, hd * HEAD_DIM:(hd + 1) * HEAD_DIM, tile_lanes(t)] = result(hd).astype(BF16)
        for hd in range(B_HEADS):
            base = hd * B_V_DIM
            u = first_b_unit + 2 * hd
            d = result(u) - lam * result(u + 1)
            ob_ref[0, base:base + B_V_DIM, tile_lanes(t)] = (_rms(d, 0) * gain).astype(BF16)

    load_queries(0, 0)
    for u in all_units:
        scores(0, 0, u, 0)
    first_block(0)

    def tile_body(t, carry):
        qslot = jnp.bitwise_and(t, 1)
        middle_blocks(qslot)
        load_queries(t + 1, 1 - qslot)
        for u in all_units:
            scores(0, 0, u, 1 - qslot)
            finish(last, last % 2, u)
        store_tile(t)
        first_block(1 - qslot)
        return carry

    if n_tiles > 1:
        lax.fori_loop(0, n_tiles - 1, tile_body, 0)
    middle_blocks((n_tiles - 1) % 2)
    for u in all_units:
        finish(last, last % 2, u, first=(n_kv == 1))
    store_tile(n_tiles - 1)


def _attention(qa, qb, ka, va, kb, vb, lamv, gs, tq, tk, n_tiles):
    b, _, s = qa.shape
    t = ka.shape[1]
    qspec = pl.BlockSpec((1, A_Q_COLS, n_tiles * tq), lambda bi, i: (bi, 0, i))
    whole = lambda a: pl.BlockSpec((1,) + a.shape[1:], lambda bi, i: (bi, 0, 0))
    full = lambda a: pl.BlockSpec(a.shape, lambda bi, i: (0,) * a.ndim)
    return pl.pallas_call(
        functools.partial(_attn_kernel, tq=tq, tk=tk, n_kv=t // tk, unroll=ATTN_UNROLL,
                          n_tiles=n_tiles),
        grid=(b, s // (n_tiles * tq)),
        in_specs=[qspec, qspec, whole(ka), whole(va), whole(kb), whole(vb), full(lamv), full(gs)],
        out_specs=[qspec, qspec],
        out_shape=[jax.ShapeDtypeStruct((b, A_Q_COLS, s), BF16),
                   jax.ShapeDtypeStruct((b, B_V_COLS, s), BF16)],
        scratch_shapes=[
            pltpu.VMEM((2, ATTN_UNITS, 2 * HEAD_DIM, tq), BF16),
            pltpu.VMEM((ATTN_UNITS, 1, tq), F32),
            pltpu.VMEM((ATTN_UNITS, B_V_DIM + ONES_ROWS, tq), F32),
            pltpu.VMEM((2, ATTN_UNITS, tk, tq), F32),
            pltpu.VMEM((2, ATTN_UNITS, 1, tq), F32),
        ],
        compiler_params=pltpu.CompilerParams(
            dimension_semantics=("arbitrary", "arbitrary"), vmem_limit_bytes=VMEM_LIMIT),
        name="attention_t%d" % t,
    )(qa, qb, ka, va, kb, vb, lamv, gs)


def _out_kernel(x_ref, mods_ref, g1_ref, g2_ref, gf_ref, oa_ref, ob_ref, wg_ref, bg_ref,
                wa_ref, wb_ref, wo_ref, w1_ref, w2_ref, y_ref):
    n_rows, tm, _ = x_ref.shape
    if n_rows > 1:
        subs = [(r, slice(None)) for r in range(n_rows)]
    elif tm % (2 * PROJ_MIN_SUBTILE) == 0:
        subs = [(0, slice(k * (tm // 2), (k + 1) * (tm // 2))) for k in range(2)]
    else:
        subs = [(0, slice(None))]
    mods = mods_ref[0]
    tn = (((0,), (0,)), ((), ()))
    dot = functools.partial(jnp.dot, preferred_element_type=F32)
    xs = [x_ref[r, tok, :] for r, tok in subs]
    hs = [_modulated_norm(x, g1_ref[...], mods[1:2], mods[0:1]).astype(BF16) for x in xs]
    gates = [jax.nn.sigmoid(dot(h, wg_ref[...]) + bg_ref[...]) for h in hs]
    br_a = [lax.dot_general(oa_ref[r, :, tok], wa_ref[...], tn, preferred_element_type=F32)
            for r, tok in subs]
    br_b = [lax.dot_general(ob_ref[r, :, tok], wb_ref[...], tn, preferred_element_type=F32)
            for r, tok in subs]
    merged = [(g[:, :D_MODEL] * a + g[:, D_MODEL:] * b).astype(BF16)
              for g, a, b in zip(gates, br_a, br_b)]
    xs = [x + mods[2:3] * dot(m, wo_ref[...]) for x, m in zip(xs, merged)]
    h2 = [_modulated_norm(x, g2_ref[...], mods[4:5], mods[3:4]).astype(BF16) for x in xs]
    f = [jnp.square(jnp.maximum(dot(h, w1_ref[...]), 0.0)).astype(BF16) for h in h2]
    xs = [x + mods[5:6] * dot(a, w2_ref[...]) for x, a in zip(xs, f)]
    for (r, tok), x in zip(subs, xs):
        y_ref[r, tok, :] = _rms(x, -1) * gf_ref[...]


def _output(x, mods, mod_row, g1, g2, gf, oa, ob, wg, bg, wa, wb, wo, w1, w2, tm, rows=1):
    b, s, _ = x.shape
    full = lambda a: pl.BlockSpec(a.shape, lambda bi, i: (0,) * a.ndim)
    xspec = pl.BlockSpec((rows, tm, D_MODEL), lambda bi, i: (bi, i, 0))
    ospec = pl.BlockSpec((rows, A_Q_COLS, tm), lambda bi, i: (bi, 0, i))
    return pl.pallas_call(
        _out_kernel,
        grid=(b // rows, s // tm),
        in_specs=[xspec,
                  pl.BlockSpec((1, N_MOD, D_MODEL), lambda bi, i: (mod_row(bi * rows), 0, 0)),
                  full(g1), full(g2), full(gf), ospec, ospec,
                  full(wg), full(bg), full(wa), full(wb), full(wo), full(w1), full(w2)],
        out_specs=xspec,
        out_shape=jax.ShapeDtypeStruct(x.shape, F32),
        compiler_params=pltpu.CompilerParams(
            dimension_semantics=("arbitrary", "arbitrary"), vmem_limit_bytes=VMEM_LIMIT),
        name="output_s%d" % s,
    )(x, mods, g1, g2, gf, oa, ob, wg, bg, wa, wb, wo, w1, w2)


def _rope_tables_t(n_tokens):
    pos = jnp.arange(n_tokens, dtype=jnp.int32)
    row = (pos // GRID_W).astype(F32)
    col = (pos % GRID_W).astype(F32)
    n_freq = HEAD_DIM // 4
    freqs = ROPE_BASE ** (-jnp.arange(n_freq, dtype=F32) / n_freq)
    ang = jnp.concatenate([freqs[:, None] * row[None, :], freqs[:, None] * col[None, :]], axis=0)
    return jnp.cos(ang), jnp.sin(ang)


def kernel(x_prompt, x_sample, cache_a_k, cache_a_v, cache_b_k, cache_b_v, c, c_ctx, w_mod, b_mod, norm1_g, w_in, a_q_norm_g, a_k_norm_g, lam_q1, lam_k1, lam_q2, lam_k2, b_subln_g, w_gate, b_gate, w_br_a, w_br_b, w_out, norm2_g, w_fc1, w_fc2, final_norm_g):
    nb, seq, _ = x_prompt.shape
    db, dseq, _ = x_sample.shape
    past = cache_a_k.shape[2]
    l = 0

    cond = jnp.concatenate([c, c_ctx[None, :]], axis=0)
    ctx_row = db
    rows = -(-cond.shape[0] // 8) * 8
    cond = jnp.pad(cond, ((0, rows - cond.shape[0]), (0, 0)))
    mods = _modulation(cond, w_mod[l], b_mod[l]).reshape(rows, N_MOD, D_MODEL)

    g1 = norm1_g[l].reshape(1, D_MODEL)
    g2 = norm2_g[l].reshape(1, D_MODEL)
    gf = final_norm_g.reshape(1, D_MODEL)
    gq = a_q_norm_g[l].reshape(HEAD_DIM, 1)
    gk = a_k_norm_g[l].reshape(HEAD_DIM, 1)
    gs = b_subln_g[l].reshape(B_V_DIM, 1)
    lamv = jnp.stack([lam_q1[l], lam_k1[l], lam_q2[l], lam_k2[l]], axis=0)
    w_in_t = w_in[l].T.astype(BF16)
    wg = w_gate[l].astype(BF16)
    bg = b_gate[l].reshape(1, -1)
    wa = w_br_a[l].astype(BF16)
    wb = w_br_b[l].astype(BF16)
    wo = w_out[l].astype(BF16)
    w1 = w_fc1[l].astype(BF16)
    w2 = w_fc2[l].astype(BF16)

    ctx_mod = lambda bi: ctx_row
    lat_mod = lambda bi: bi

    tm_c = min(seq, 512)
    qa, ka, va, qb, kb, vb, new_ak, new_av, new_bk, new_bv = _projection(
        x_prompt, mods, ctx_mod, g1, w_in_t, gq, gk, None, True, tm_c)
    oa, ob = _attention(qa, qb, ka, va, kb, vb, lamv, gs, tq=min(seq, 256), tk=min(seq, 256),
                        n_tiles=1)
    y_prompt = _output(x_prompt, mods, ctx_mod, g1, g2, gf, oa, ob, wg, bg, wa, wb, wo, w1, w2, tm_c,
                       rows=2 if nb % 2 == 0 else 1)

    rope = _rope_tables_t(dseq)
    cached = (cache_a_k[:, l].reshape(db, past, A_KV_COLS), cache_a_v[:, l].reshape(db, past, A_KV_COLS),
              cache_b_k[:, l].reshape(db, past, B_QK_COLS), cache_b_v[:, l].reshape(db, past, B_V_COLS))
    qa, ka, va, qb, kb, vb = _projection(
        x_sample, mods, lat_mod, g1, w_in_t, gq, gk, rope, False, past, past=cached)
    oa, ob = _attention(qa, qb, ka, va, kb, vb, lamv, gs, tq=256, tk=256, n_tiles=4)
    y_sample = _output(x_sample, mods, lat_mod, g1, g2, gf, oa, ob, wg, bg, wa, wb, wo, w1, w2, 512)

    return (y_prompt, y_sample,
            new_ak.reshape(nb, 1, seq, A_KV_HEADS, HEAD_DIM),
            new_av.reshape(nb, 1, seq, A_KV_HEADS, HEAD_DIM),
            new_bk.reshape(nb, 1, seq, B_HEADS, 2, HEAD_DIM),
            new_bv.reshape(nb, 1, seq, B_HEADS, B_V_DIM))
```

```python
import functools
from typing import NamedTuple

import jax
import jax.numpy as jnp
from jax import lax
from jax.experimental import pallas as pl
from jax.experimental.pallas import tpu as pltpu

F32 = jnp.float32
BF16 = jnp.bfloat16

D_MODEL = 1024
HEAD_DIM = 64
HALF = HEAD_DIM // 2
GRID_W = 64
A_Q_HEADS = 8
A_KV_HEADS = 2
A_GROUP = A_Q_HEADS // A_KV_HEADS
B_HEADS = 4
B_V_DIM = 2 * HEAD_DIM
A_Q_COLS = A_Q_HEADS * HEAD_DIM
A_KV_COLS = A_KV_HEADS * HEAD_DIM
B_QK_COLS = B_HEADS * 2 * HEAD_DIM
B_V_COLS = B_HEADS * B_V_DIM
OFF_AQ = 0
OFF_AK = OFF_AQ + A_Q_COLS
OFF_AV = OFF_AK + A_KV_COLS
OFF_BQ = OFF_AV + A_KV_COLS
OFF_BK = OFF_BQ + B_QK_COLS
OFF_BV = OFF_BK + B_QK_COLS
IN_COLS = OFF_BV + B_V_COLS
D_FF = 4 * D_MODEL
N_MOD = 6
ROPE_BASE = 10000.0
EPS = 1e-6
LAM_INIT = 0.2
LOG2E = 1.4426950408889634
Q_PRESCALE = HEAD_DIM ** -0.5 * LOG2E

V7X_VMEM_BYTES = 64 * 1024 * 1024
VMEM_LIMIT = 56 * 1024 * 1024


def _rms(x, axis):
    return x * lax.rsqrt(jnp.mean(x * x, axis=axis, keepdims=True) + EPS)


def _modulated_norm(x, g, scale, shift):
    return (_rms(x, -1) * g) * (1.0 + scale) + shift


def _mod_kernel(cond_ref, w_ref, b_ref, o_ref):
    c = cond_ref[...]
    s = c * jax.nn.sigmoid(c)
    o_ref[...] = jnp.dot(s.astype(BF16), w_ref[...].astype(BF16),
                         preferred_element_type=F32) + b_ref[...]


def _modulation(cond, w_mod, b_mod):
    rows = cond.shape[0]
    n = w_mod.shape[1]
    tn = 1536
    return pl.pallas_call(
        _mod_kernel,
        grid=(n // tn,),
        in_specs=[
            pl.BlockSpec((rows, D_MODEL), lambda j: (0, 0)),
            pl.BlockSpec((D_MODEL, tn), lambda j: (0, j)),
            pl.BlockSpec((1, tn), lambda j: (0, j)),
        ],
        out_specs=pl.BlockSpec((rows, tn), lambda j: (0, j)),
        out_shape=jax.ShapeDtypeStruct((rows, n), F32),
        compiler_params=pltpu.CompilerParams(
            dimension_semantics=("arbitrary",), vmem_limit_bytes=VMEM_LIMIT),
        name="modulation",
    )(cond, w_mod, b_mod.reshape(1, n))


PROJ_MIN_SUBTILE = 256
PROJ_ROW_CHUNKS = ((OFF_AQ, OFF_AK), (OFF_AK, OFF_BK), (OFF_BK, OFF_BV), (OFF_BV, IN_COLS))


def _rope_t(x, cos, sin):
    x1, x2 = x[:HALF], x[HALF:]
    return jnp.concatenate([x1 * cos - x2 * sin, x1 * sin + x2 * cos], axis=0)


def _proj_kernel(*refs, use_rope, emit_cache, with_past):
    it = iter(refs)
    x_ref, mods_ref, g1_ref, w_ref, gq_ref, gk_ref = (next(it) for _ in range(6))
    cos_ref = sin_ref = None
    if use_rope:
        cos_ref, sin_ref = next(it), next(it)
    if with_past:
        pak_ref, pav_ref, pbk_ref, pbv_ref = (next(it) for _ in range(4))
    qa_ref, ka_ref, va_ref, qb_ref, kb_ref, vb_ref = (next(it) for _ in range(6))
    if emit_cache:
        cak_ref, cav_ref, cbk_ref, cbv_ref = (next(it) for _ in range(4))

    tm = x_ref.shape[1]
    n_sub = 2 if tm % (2 * PROJ_MIN_SUBTILE) == 0 else 1
    ts = tm // n_sub

    def project():
        mods = mods_ref[0]
        g1 = g1_ref[...]
        nt = (((1,), (1,)), ((), ()))
        t = []
        for s in range(n_sub):
            x = x_ref[0, s * ts:(s + 1) * ts, :]
            h = _modulated_norm(x, g1, mods[1:2], mods[0:1]).astype(BF16)
            t.append([lax.dot_general(w_ref[r0:r1, :], h, nt, preferred_element_type=F32)
                      for r0, r1 in PROJ_ROW_CHUNKS])
        gq, gk = gq_ref[...], gk_ref[...]

        for s in range(n_sub):
            cols = slice(s * ts, (s + 1) * ts)
            if use_rope:
                cos, sin = cos_ref[:, cols], sin_ref[:, cols]

            def rows(r0, r1):
                for c, (c0, c1) in enumerate(PROJ_ROW_CHUNKS):
                    if c0 <= r0 and r1 <= c1:
                        return t[s][c][r0 - c0:r1 - c0]
                raise ValueError("row range crosses a projection chunk")

            def head(off, gain=None):
                blk = rows(off, off + HEAD_DIM)
                if gain is not None:
                    blk = _rms(blk, 0) * gain
                if use_rope:
                    blk = _rope_t(blk, cos, sin)
                return blk

            for i in range(A_Q_HEADS):
                off = OFF_AQ + i * HEAD_DIM
                qa_ref[0, i * HEAD_DIM:(i + 1) * HEAD_DIM, cols] = (
                    head(off, gq) * Q_PRESCALE).astype(BF16)
            ak_t = jnp.concatenate(
                [head(OFF_AK + i * HEAD_DIM, gk) for i in range(A_KV_HEADS)], axis=0)
            ak = ak_t.T
            ka_ref[0, cols, :] = ak.astype(BF16)
            av_t = rows(OFF_AV, OFF_AV + A_KV_COLS)
            va_ref[0, :, cols] = av_t.astype(BF16)
            for i in range(2 * B_HEADS):
                off = OFF_BQ + i * HEAD_DIM
                qb_ref[0, i * HEAD_DIM:(i + 1) * HEAD_DIM, cols] = (
                    head(off) * Q_PRESCALE).astype(BF16)
            bk_t = jnp.concatenate(
                [head(OFF_BK + i * HEAD_DIM) for i in range(2 * B_HEADS)], axis=0)
            bk = bk_t.T
            kb_ref[0, cols, :] = bk.astype(BF16)
            bv_t = rows(OFF_BV, OFF_BV + B_V_COLS)
            vb_ref[0, :, cols] = bv_t.astype(BF16)
            if emit_cache:
                cak_ref[0, cols, :] = ak
                cbk_ref[0, cols, :] = bk
                cav_ref[0, cols, :] = av_t.T
                cbv_ref[0, cols, :] = bv_t.T

    if not with_past:
        project()
        return

    step = pl.program_id(1)

    @pl.when(step == 0)
    def _():
        ka_ref[0] = pak_ref[0].astype(BF16)
        kb_ref[0] = pbk_ref[0].astype(BF16)
        va_ref[0] = pav_ref[0].T.astype(BF16)
        vb_ref[0] = pbv_ref[0].T.astype(BF16)

    pl.when(step > 0)(project)


def _projection(x, mods, mod_row, g1, w_in_t, gq, gk, rope, emit_cache, tm, past=None):
    b, s, _ = x.shape
    use_rope = rope is not None
    with_past = past is not None
    lead = 1 if with_past else 0
    if with_past:
        assert all(p.shape[1] == tm for p in past), "cached context must fill exactly one key block"
    tok = lambda i: jnp.maximum(i - lead, 0)
    full = lambda shape: pl.BlockSpec(shape, lambda bi, i: (0,) * len(shape))
    in_specs = [
        pl.BlockSpec((1, tm, D_MODEL), lambda bi, i: (bi, tok(i), 0)),
        pl.BlockSpec((1, N_MOD, D_MODEL), lambda bi, i: (mod_row(bi), 0, 0)),
        full((1, D_MODEL)),
        full((IN_COLS, D_MODEL)),
        full((HEAD_DIM, 1)),
        full((HEAD_DIM, 1)),
    ]
    args = [x, mods, g1, w_in_t, gq, gk]
    if use_rope:
        in_specs += [pl.BlockSpec((HALF, tm), lambda bi, i: (0, tok(i)))] * 2
        args += list(rope)
    if with_past:
        in_specs += [pl.BlockSpec((1,) + p.shape[1:], lambda bi, i: (bi, 0, 0)) for p in past]
        args += list(past)
    keys = s + lead * tm
    q_spec = lambda rows: pl.BlockSpec((1, rows, tm), lambda bi, i: (bi, 0, tok(i)))
    fm = lambda rows: pl.BlockSpec((1, rows, tm), lambda bi, i: (bi, 0, i))
    tk = lambda cols: pl.BlockSpec((1, tm, cols), lambda bi, i: (bi, i, 0))
    out_specs = [q_spec(A_Q_COLS), tk(A_KV_COLS), fm(A_KV_COLS),
                 q_spec(B_QK_COLS), tk(B_QK_COLS), fm(B_V_COLS)]
    out_shape = [
        jax.ShapeDtypeStruct((b, A_Q_COLS, s), BF16),
        jax.ShapeDtypeStruct((b, keys, A_KV_COLS), BF16),
        jax.ShapeDtypeStruct((b, A_KV_COLS, keys), BF16),
        jax.ShapeDtypeStruct((b, B_QK_COLS, s), BF16),
        jax.ShapeDtypeStruct((b, keys, B_QK_COLS), BF16),
        jax.ShapeDtypeStruct((b, B_V_COLS, keys), BF16),
    ]
    if emit_cache:
        assert not with_past
        out_specs += [tk(A_KV_COLS), tk(A_KV_COLS), tk(B_QK_COLS), tk(B_V_COLS)]
        out_shape += [
            jax.ShapeDtypeStruct((b, s, A_KV_COLS), F32),
            jax.ShapeDtypeStruct((b, s, A_KV_COLS), F32),
            jax.ShapeDtypeStruct((b, s, B_QK_COLS), F32),
            jax.ShapeDtypeStruct((b, s, B_V_COLS), F32),
        ]
    return pl.pallas_call(
        functools.partial(_proj_kernel, use_rope=use_rope, emit_cache=emit_cache,
                          with_past=with_past),
        grid=(b, s // tm + lead),
        in_specs=in_specs,
        out_specs=out_specs,
        out_shape=out_shape,
        compiler_params=pltpu.CompilerParams(
            dimension_semantics=("arbitrary", "arbitrary"), vmem_limit_bytes=VMEM_LIMIT),
        name="projection_rope" if use_rope else "projection_ctx",
    )(*args)


ONES_ROWS = 16
ATTN_UNITS = A_Q_HEADS + 2 * B_HEADS
ATTN_UNROLL = 8


def _attn_kernel(qa_ref, qb_ref, ka_ref, va_ref, kb_ref, vb_ref, lamv_ref, gs_ref,
                 oa_ref, ob_ref, qpad_sc, m_sc, acc_sc, s_sc, bm_sc, *, tq, tk, n_kv, unroll, n_tiles):
    zeros = jnp.zeros((HEAD_DIM, tq), BF16)
    ones = jnp.ones((ONES_ROWS, tk), BF16)

    units = []
    for hd in range(A_Q_HEADS):
        g = hd // A_GROUP
        units.append((qa_ref, slice(hd * HEAD_DIM, (hd + 1) * HEAD_DIM), g,
                      ka_ref, slice(None), va_ref, slice(g * HEAD_DIM, (g + 1) * HEAD_DIM), HEAD_DIM))
    for hd in range(B_HEADS):
        base = hd * B_V_DIM
        for c in range(2):
            units.append((qb_ref, slice(base + c * HEAD_DIM, base + (c + 1) * HEAD_DIM), c,
                          kb_ref, slice(base, base + B_V_DIM), vb_ref, slice(base, base + B_V_DIM),
                          B_V_DIM))
    first_b_unit = A_Q_HEADS
    all_units = range(len(units))

    def tile_lanes(t):
        if isinstance(t, int):
            return pl.ds(t * tq, tq)
        return pl.ds(pl.multiple_of(t * tq, tq), tq)

    def load_queries(t, qslot):
        for u in all_units:
            q_ref, q_rows, half = units[u][:3]
            pieces = [zeros, zeros]
            pieces[half] = q_ref[0, q_rows, tile_lanes(t)]
            qpad_sc[qslot, u] = jnp.concatenate(pieces, axis=0)

    def block_offset(j):
        if isinstance(j, int):
            return j * tk
        return pl.multiple_of(j * tk, tk)

    def scores(j, slot, u, qslot):
        k_ref, k_lanes = units[u][3:5]
        kb = k_ref[0, pl.ds(block_offset(j), tk), k_lanes]
        s = jnp.dot(kb, qpad_sc[qslot, u], preferred_element_type=F32)
        s_sc[slot, u] = s
        bm_sc[slot, u] = jnp.max(s, axis=0, keepdims=True)

    def finish(j, slot, u, first=False):
        v_ref, v_rows, dv = units[u][5:]
        vext = jnp.concatenate([v_ref[0, v_rows, pl.ds(block_offset(j), tk)], ones], axis=0)
        rows = dv + ONES_ROWS
        if first:
            m_new = bm_sc[slot, u]
        else:
            m_prev = m_sc[u]
            m_new = jnp.maximum(m_prev, bm_sc[slot, u])
        p = jnp.exp2(s_sc[slot, u] - m_new).astype(BF16)
        pv = jnp.dot(vext, p, preferred_element_type=F32)
        if first:
            acc_sc[u, :rows] = pv
        else:
            acc_sc[u, :rows] = jnp.exp2(m_prev - m_new) * acc_sc[u, :rows] + pv
        m_sc[u] = m_new

    assert unroll % 2 == 0
    last = n_kv - 1
    assert n_tiles == 1 or last % 2 == 1, "next tile's block 0 reuses slot 0"

    def advance(j, slot, qslot, first=False):
        for u in all_units:
            scores(j + 1, 1 - slot, u, qslot)
            finish(j, slot, u, first)

    def first_block(qslot):
        if n_kv > 1:
            advance(0, 0, qslot, first=True)

    def middle_blocks(qslot):
        def body(i, carry):
            for k in range(unroll):
                advance(1 + unroll * i + k, (1 + k) % 2, qslot)
            return carry

        n_iter = max(n_kv - 2, 0) // unroll
        if n_iter > 0:
            lax.fori_loop(0, n_iter, body, 0)
        for j in range(1 + unroll * n_iter, last):
            advance(j, j % 2, qslot)

    lv = lamv_ref[...]
    lam = (jnp.exp(jnp.sum(lv[0:1] * lv[1:2], axis=-1, keepdims=True))
           - jnp.exp(jnp.sum(lv[2:3] * lv[3:4], axis=-1, keepdims=True)) + LAM_INIT)
    gain = gs_ref[...] * (1.0 - LAM_INIT)

    def result(u):
        dv = units[u][7]
        return acc_sc[u, :dv] * (1.0 / acc_sc[u, dv:dv + 1])

    def store_tile(t):
        for hd in range(A_Q_HEADS):
            oa_ref[0, hd * HEAD_DIM:(hd + 1) * HEAD_DIM, tile_lanes(t)] = result(hd).astype(BF16)
        for hd in range(B_HEADS):
            base = hd * B_V_DIM
            u = first_b_unit + 2 * hd
            d = result(u) - lam * result(u + 1)
            ob_ref[0, base:base + B_V_DIM, tile_lanes(t)] = (_rms(d, 0) * gain).astype(BF16)

    load_queries(0, 0)
    for u in all_units:
        scores(0, 0, u, 0)
    first_block(0)

    def tile_body(t, carry):
        qslot = jnp.bitwise_and(t, 1)
        middle_blocks(qslot)
        load_queries(t + 1, 1 - qslot)
        for u in all_units:
            scores(0, 0, u, 1 - qslot)
            finish(last, last % 2, u)
        store_tile(t)
        first_block(1 - qslot)
        return carry

    if n_tiles > 1:
        lax.fori_loop(0, n_tiles - 1, tile_body, 0)
    middle_blocks((n_tiles - 1) % 2)
    for u in all_units:
        finish(last, last % 2, u, first=(n_kv == 1))
    store_tile(n_tiles - 1)


def _attention(qa, qb, ka, va, kb, vb, lamv, gs, tq, tk, n_tiles):
    b, _, s = qa.shape
    t = ka.shape[1]
    qspec = pl.BlockSpec((1, A_Q_COLS, n_tiles * tq), lambda bi, i: (bi, 0, i))
    whole = lambda a: pl.BlockSpec((1,) + a.shape[1:], lambda bi, i: (bi, 0, 0))
    full = lambda a: pl.BlockSpec(a.shape, lambda bi, i: (0,) * a.ndim)
    return pl.pallas_call(
        functools.partial(_attn_kernel, tq=tq, tk=tk, n_kv=t // tk, unroll=ATTN_UNROLL,
                          n_tiles=n_tiles),
        grid=(b, s // (n_tiles * tq)),
        in_specs=[qspec, qspec, whole(ka), whole(va), whole(kb), whole(vb), full(lamv), full(gs)],
        out_specs=[qspec, qspec],
        out_shape=[jax.ShapeDtypeStruct((b, A_Q_COLS, s), BF16),
                   jax.ShapeDtypeStruct((b, B_V_COLS, s), BF16)],
        scratch_shapes=[
            pltpu.VMEM((2, ATTN_UNITS, 2 * HEAD_DIM, tq), BF16),
            pltpu.VMEM((ATTN_UNITS, 1, tq), F32),
            pltpu.VMEM((ATTN_UNITS, B_V_DIM + ONES_ROWS, tq), F32),
            pltpu.VMEM((2, ATTN_UNITS, tk, tq), F32),
            pltpu.VMEM((2, ATTN_UNITS, 1, tq), F32),
        ],
        compiler_params=pltpu.CompilerParams(
            dimension_semantics=("arbitrary", "arbitrary"), vmem_limit_bytes=VMEM_LIMIT),
        name="attention_t%d" % t,
    )(qa, qb, ka, va, kb, vb, lamv, gs)


def _out_kernel(x_ref, mods_ref, g1_ref, g2_ref, gf_ref, oa_ref, ob_ref, wg_ref, bg_ref,
                wa_ref, wb_ref, wo_ref, w1_ref, w2_ref, y_ref):
    n_rows, tm, _ = x_ref.shape
    if n_rows > 1:
        subs = [(r, slice(None)) for r in range(n_rows)]
    elif tm % (2 * PROJ_MIN_SUBTILE) == 0:
        subs = [(0, slice(k * (tm // 2), (k + 1) * (tm // 2))) for k in range(2)]
    else:
        subs = [(0, slice(None))]
    mods = mods_ref[0]
    tn = (((0,), (0,)), ((), ()))
    dot = functools.partial(jnp.dot, preferred_element_type=F32)
    xs = [x_ref[r, tok, :] for r, tok in subs]
    hs = [_modulated_norm(x, g1_ref[...], mods[1:2], mods[0:1]).astype(BF16) for x in xs]
    gates = [jax.nn.sigmoid(dot(h, wg_ref[...]) + bg_ref[...]) for h in hs]
    br_a = [lax.dot_general(oa_ref[r, :, tok], wa_ref[...], tn, preferred_element_type=F32)
            for r, tok in subs]
    br_b = [lax.dot_general(ob_ref[r, :, tok], wb_ref[...], tn, preferred_element_type=F32)
            for r, tok in subs]
    merged = [(g[:, :D_MODEL] * a + g[:, D_MODEL:] * b).astype(BF16)
              for g, a, b in zip(gates, br_a, br_b)]
    xs = [x + mods[2:3] * dot(m, wo_ref[...]) for x, m in zip(xs, merged)]
    h2 = [_modulated_norm(x, g2_ref[...], mods[4:5], mods[3:4]).astype(BF16) for x in xs]
    f = [jnp.square(jnp.maximum(dot(h, w1_ref[...]), 0.0)).astype(BF16) for h in h2]
    xs = [x + mods[5:6] * dot(a, w2_ref[...]) for x, a in zip(xs, f)]
    for (r, tok), x in zip(subs, xs):
        y_ref[r, tok, :] = _rms(x, -1) * gf_ref[...]


def _output(x, mods, mod_row, g1, g2, gf, oa, ob, wg, bg, wa, wb, wo, w1, w2, tm, rows=1):
    b, s, _ = x.shape
    full = lambda a: pl.BlockSpec(a.shape, lambda bi, i: (0,) * a.ndim)
    xspec = pl.BlockSpec((rows, tm, D_MODEL), lambda bi, i: (bi, i, 0))
    ospec = pl.BlockSpec((rows, A_Q_COLS, tm), lambda bi, i: (bi, 0, i))
    return pl.pallas_call(
        _out_kernel,
        grid=(b // rows, s // tm),
        in_specs=[xspec,
                  pl.BlockSpec((1, N_MOD, D_MODEL), lambda bi, i: (mod_row(bi * rows), 0, 0)),
                  full(g1), full(g2), full(gf), ospec, ospec,
                  full(wg), full(bg), full(wa), full(wb), full(wo), full(w1), full(w2)],
        out_specs=xspec,
        out_shape=jax.ShapeDtypeStruct(x.shape, F32),
        compiler_params=pltpu.CompilerParams(
            dimension_semantics=("arbitrary", "arbitrary"), vmem_limit_bytes=VMEM_LIMIT),
        name="output_s%d" % s,
    )(x, mods, g1, g2, gf, oa, ob, wg, bg, wa, wb, wo, w1, w2)


TOKEN_TILE = 512
ATTN_QUERY_TILE = PROJ_MIN_SUBTILE
ATTN_KEY_BLOCK = 256
ATTN_MAX_TILES = 8


class _Tiles(NamedTuple):
    tm: int
    tq: int
    tk: int
    n_tiles: int


def _tiles(n_tokens, n_keys):
    tm = min(n_tokens, TOKEN_TILE)
    tq = min(n_tokens, ATTN_QUERY_TILE)
    tk = min(n_keys, ATTN_KEY_BLOCK)
    assert n_tokens % tm == 0 and n_tokens % tq == 0 and n_keys % tk == 0
    q_tiles = n_tokens // tq
    n_tiles = 1
    if (n_keys // tk) % 2 == 0:
        n_tiles = max(d for d in range(1, ATTN_MAX_TILES + 1) if q_tiles % d == 0)
    return _Tiles(tm, tq, tk, n_tiles)


def _rope_tables_t(n_tokens):
    pos = jnp.arange(n_tokens, dtype=jnp.int32)
    row = (pos // GRID_W).astype(F32)
    col = (pos % GRID_W).astype(F32)
    n_freq = HEAD_DIM // 4
    freqs = ROPE_BASE ** (-jnp.arange(n_freq, dtype=F32) / n_freq)
    ang = jnp.concatenate([freqs[:, None] * row[None, :], freqs[:, None] * col[None, :]], axis=0)
    return jnp.cos(ang), jnp.sin(ang)


def kernel(x_prompt, x_sample, cache_a_k, cache_a_v, cache_b_k, cache_b_v, c, c_ctx, w_mod, b_mod, norm1_g, w_in, a_q_norm_g, a_k_norm_g, lam_q1, lam_k1, lam_q2, lam_k2, b_subln_g, w_gate, b_gate, w_br_a, w_br_b, w_out, norm2_g, w_fc1, w_fc2, final_norm_g):
    nb, seq, _ = x_prompt.shape
    db, dseq, _ = x_sample.shape
    past = cache_a_k.shape[2]
    l = 0

    cond = jnp.concatenate([c, c_ctx[None, :]], axis=0)
    ctx_row = db
    rows = -(-cond.shape[0] // 8) * 8
    cond = jnp.pad(cond, ((0, rows - cond.shape[0]), (0, 0)))
    mods = _modulation(cond, w_mod[l], b_mod[l]).reshape(rows, N_MOD, D_MODEL)

    g1 = norm1_g[l].reshape(1, D_MODEL)
    g2 = norm2_g[l].reshape(1, D_MODEL)
    gf = final_norm_g.reshape(1, D_MODEL)
    gq = a_q_norm_g[l].reshape(HEAD_DIM, 1)
    gk = a_k_norm_g[l].reshape(HEAD_DIM, 1)
    gs = b_subln_g[l].reshape(B_V_DIM, 1)
    lamv = jnp.stack([lam_q1[l], lam_k1[l], lam_q2[l], lam_k2[l]], axis=0)
    w_in_t = w_in[l].T.astype(BF16)
    wg = w_gate[l].astype(BF16)
    bg = b_gate[l].reshape(1, -1)
    wa = w_br_a[l].astype(BF16)
    wb = w_br_b[l].astype(BF16)
    wo = w_out[l].astype(BF16)
    w1 = w_fc1[l].astype(BF16)
    w2 = w_fc2[l].astype(BF16)

    ctx_mod = lambda bi: ctx_row
    lat_mod = lambda bi: bi

    tiles = _tiles(seq, seq)
    qa, ka, va, qb, kb, vb, new_ak, new_av, new_bk, new_bv = _projection(
        x_prompt, mods, ctx_mod, g1, w_in_t, gq, gk, None, True, tiles.tm)
    oa, ob = _attention(qa, qb, ka, va, kb, vb, lamv, gs, tiles.tq, tiles.tk, tiles.n_tiles)
    batch_rows = TOKEN_TILE // tiles.tm
    if nb % batch_rows:
        batch_rows = 1
    y_prompt = _output(x_prompt, mods, ctx_mod, g1, g2, gf, oa, ob, wg, bg, wa, wb, wo, w1, w2,
                       tiles.tm, rows=batch_rows)

    tiles = _tiles(dseq, past + dseq)
    assert tiles.tm == past, "the cached context must fill exactly one projection tile"
    rope = _rope_tables_t(dseq)
    cached = (cache_a_k[:, l].reshape(db, past, A_KV_COLS), cache_a_v[:, l].reshape(db, past, A_KV_COLS),
              cache_b_k[:, l].reshape(db, past, B_QK_COLS), cache_b_v[:, l].reshape(db, past, B_V_COLS))
    qa, ka, va, qb, kb, vb = _projection(
        x_sample, mods, lat_mod, g1, w_in_t, gq, gk, rope, False, tiles.tm, past=cached)
    oa, ob = _attention(qa, qb, ka, va, kb, vb, lamv, gs, tiles.tq, tiles.tk, tiles.n_tiles)
    y_sample = _output(x_sample, mods, lat_mod, g1, g2, gf, oa, ob, wg, bg, wa, wb, wo, w1, w2,
                       tiles.tm)

    return (y_prompt, y_sample,
            new_ak.reshape(nb, 1, seq, A_KV_HEADS, HEAD_DIM),
            new_av.reshape(nb, 1, seq, A_KV_HEADS, HEAD_DIM),
            new_bk.reshape(nb, 1, seq, B_HEADS, 2, HEAD_DIM),
            new_bv.reshape(nb, 1, seq, B_HEADS, B_V_DIM))
```

```python
import functools
from typing import NamedTuple

import jax
import jax.numpy as jnp
from jax import lax
from jax.experimental import pallas as pl
from jax.experimental.pallas import tpu as pltpu

F32 = jnp.float32
BF16 = jnp.bfloat16

D_MODEL = 1024
HEAD_DIM = 64
HALF = HEAD_DIM // 2
GRID_W = 64
A_Q_HEADS = 8
A_KV_HEADS = 2
A_GROUP = A_Q_HEADS // A_KV_HEADS
B_HEADS = 4
B_V_DIM = 2 * HEAD_DIM
A_Q_COLS = A_Q_HEADS * HEAD_DIM
A_KV_COLS = A_KV_HEADS * HEAD_DIM
B_QK_COLS = B_HEADS * 2 * HEAD_DIM
B_V_COLS = B_HEADS * B_V_DIM
OFF_AQ = 0
OFF_AK = OFF_AQ + A_Q_COLS
OFF_AV = OFF_AK + A_KV_COLS
OFF_BQ = OFF_AV + A_KV_COLS
OFF_BK = OFF_BQ + B_QK_COLS
OFF_BV = OFF_BK + B_QK_COLS
IN_COLS = OFF_BV + B_V_COLS
D_FF = 4 * D_MODEL
N_MOD = 6
ROPE_BASE = 10000.0
EPS = 1e-6
LAM_INIT = 0.2
LOG2E = 1.4426950408889634
Q_PRESCALE = HEAD_DIM ** -0.5 * LOG2E

V7X_VMEM_BYTES = 64 * 1024 * 1024
VMEM_LIMIT = V7X_VMEM_BYTES - 8 * 1024 * 1024


def _rms(x, axis):
    return x * lax.rsqrt(jnp.mean(x * x, axis=axis, keepdims=True) + EPS)


def _modulated_norm(x, g, scale, shift):
    return (_rms(x, -1) * g) * (1.0 + scale) + shift


def _mod_kernel(cond_ref, w_ref, b_ref, o_ref):
    c = cond_ref[...]
    s = c * jax.nn.sigmoid(c)
    o_ref[...] = jnp.dot(s.astype(BF16), w_ref[...].astype(BF16),
                         preferred_element_type=F32) + b_ref[...]


def _modulation(cond, w_mod, b_mod):
    rows = cond.shape[0]
    n = w_mod.shape[1]
    tn = 1536
    return pl.pallas_call(
        _mod_kernel,
        grid=(n // tn,),
        in_specs=[
            pl.BlockSpec((rows, D_MODEL), lambda j: (0, 0)),
            pl.BlockSpec((D_MODEL, tn), lambda j: (0, j)),
            pl.BlockSpec((1, tn), lambda j: (0, j)),
        ],
        out_specs=pl.BlockSpec((rows, tn), lambda j: (0, j)),
        out_shape=jax.ShapeDtypeStruct((rows, n), F32),
        compiler_params=pltpu.CompilerParams(
            dimension_semantics=("arbitrary",), vmem_limit_bytes=VMEM_LIMIT),
        name="modulation",
    )(cond, w_mod, b_mod.reshape(1, n))


PROJ_MIN_SUBTILE = 256
PROJ_ROW_CHUNKS = ((OFF_AQ, OFF_AK), (OFF_AK, OFF_BK), (OFF_BK, OFF_BV), (OFF_BV, IN_COLS))


def _rope_t(x, cos, sin):
    x1, x2 = x[:HALF], x[HALF:]
    return jnp.concatenate([x1 * cos - x2 * sin, x1 * sin + x2 * cos], axis=0)


def _proj_kernel(*refs, use_rope, emit_cache, with_past):
    it = iter(refs)
    x_ref, mods_ref, g1_ref, w_ref, gq_ref, gk_ref = (next(it) for _ in range(6))
    cos_ref = sin_ref = None
    if use_rope:
        cos_ref, sin_ref = next(it), next(it)
    if with_past:
        pak_ref, pav_ref, pbk_ref, pbv_ref = (next(it) for _ in range(4))
    qa_ref, ka_ref, va_ref, qb_ref, kb_ref, vb_ref = (next(it) for _ in range(6))
    if emit_cache:
        cak_ref, cav_ref, cbk_ref, cbv_ref = (next(it) for _ in range(4))

    n_rows, tm, _ = x_ref.shape
    if n_rows > 1:
        subs = [(r, slice(None)) for r in range(n_rows)]
    elif tm % (2 * PROJ_MIN_SUBTILE) == 0:
        subs = [(0, slice(k * (tm // 2), (k + 1) * (tm // 2))) for k in range(2)]
    else:
        subs = [(0, slice(None))]

    def project():
        mods = mods_ref[0]
        g1 = g1_ref[...]
        nt = (((1,), (1,)), ((), ()))
        t = []
        for b, cols in subs:
            x = x_ref[b, cols, :]
            h = _modulated_norm(x, g1, mods[1:2], mods[0:1]).astype(BF16)
            t.append([lax.dot_general(w_ref[r0:r1, :], h, nt, preferred_element_type=F32)
                      for r0, r1 in PROJ_ROW_CHUNKS])
        gq, gk = gq_ref[...], gk_ref[...]

        for s, (b, cols) in enumerate(subs):
            if use_rope:
                cos, sin = cos_ref[:, cols], sin_ref[:, cols]

            def rows(r0, r1):
                for c, (c0, c1) in enumerate(PROJ_ROW_CHUNKS):
                    if c0 <= r0 and r1 <= c1:
                        return t[s][c][r0 - c0:r1 - c0]
                raise ValueError("row range crosses a projection chunk")

            def head(off, gain=None):
                blk = rows(off, off + HEAD_DIM)
                if gain is not None:
                    blk = _rms(blk, 0) * gain
                if use_rope:
                    blk = _rope_t(blk, cos, sin)
                return blk

            for i in range(A_Q_HEADS):
                off = OFF_AQ + i * HEAD_DIM
                qa_ref[b, i * HEAD_DIM:(i + 1) * HEAD_DIM, cols] = (
                    head(off, gq) * Q_PRESCALE).astype(BF16)
            ak_t = jnp.concatenate(
                [head(OFF_AK + i * HEAD_DIM, gk) for i in range(A_KV_HEADS)], axis=0)
            ak = ak_t.T
            ka_ref[b, cols, :] = ak.astype(BF16)
            av_t = rows(OFF_AV, OFF_AV + A_KV_COLS)
            va_ref[b, :, cols] = av_t.astype(BF16)
            for i in range(2 * B_HEADS):
                off = OFF_BQ + i * HEAD_DIM
                qb_ref[b, i * HEAD_DIM:(i + 1) * HEAD_DIM, cols] = (
                    head(off) * Q_PRESCALE).astype(BF16)
            bk_t = jnp.concatenate(
                [head(OFF_BK + i * HEAD_DIM) for i in range(2 * B_HEADS)], axis=0)
            bk = bk_t.T
            kb_ref[b, cols, :] = bk.astype(BF16)
            bv_t = rows(OFF_BV, OFF_BV + B_V_COLS)
            vb_ref[b, :, cols] = bv_t.astype(BF16)
            if emit_cache:
                cak_ref[b, cols, :] = ak
                cbk_ref[b, cols, :] = bk
                cav_ref[b, cols, :] = av_t.T
                cbv_ref[b, cols, :] = bv_t.T

    if not with_past:
        project()
        return

    step = pl.program_id(1)

    @pl.when(step == 0)
    def _():
        ka_ref[0] = pak_ref[0].astype(BF16)
        kb_ref[0] = pbk_ref[0].astype(BF16)
        va_ref[0] = pav_ref[0].T.astype(BF16)
        vb_ref[0] = pbv_ref[0].T.astype(BF16)

    pl.when(step > 0)(project)


def _projection(x, mods, mod_row, g1, w_in_t, gq, gk, rope, emit_cache, tm, past=None, rows=1):
    b, s, _ = x.shape
    use_rope = rope is not None
    with_past = past is not None
    lead = 1 if with_past else 0
    if with_past:
        assert rows == 1
        assert all(p.shape[1] == tm for p in past), "cached context must fill exactly one key block"
    tok = lambda i: jnp.maximum(i - lead, 0)
    full = lambda shape: pl.BlockSpec(shape, lambda bi, i: (0,) * len(shape))
    in_specs = [
        pl.BlockSpec((rows, tm, D_MODEL), lambda bi, i: (bi, tok(i), 0)),
        pl.BlockSpec((1, N_MOD, D_MODEL), lambda bi, i: (mod_row(bi * rows), 0, 0)),
        full((1, D_MODEL)),
        full((IN_COLS, D_MODEL)),
        full((HEAD_DIM, 1)),
        full((HEAD_DIM, 1)),
    ]
    args = [x, mods, g1, w_in_t, gq, gk]
    if use_rope:
        in_specs += [pl.BlockSpec((HALF, tm), lambda bi, i: (0, tok(i)))] * 2
        args += list(rope)
    if with_past:
        in_specs += [pl.BlockSpec((1,) + p.shape[1:], lambda bi, i: (bi, 0, 0)) for p in past]
        args += list(past)
    keys = s + lead * tm
    q_spec = lambda feat: pl.BlockSpec((rows, feat, tm), lambda bi, i: (bi, 0, tok(i)))
    fm = lambda feat: pl.BlockSpec((rows, feat, tm), lambda bi, i: (bi, 0, i))
    tk = lambda cols: pl.BlockSpec((rows, tm, cols), lambda bi, i: (bi, i, 0))
    out_specs = [q_spec(A_Q_COLS), tk(A_KV_COLS), fm(A_KV_COLS),
                 q_spec(B_QK_COLS), tk(B_QK_COLS), fm(B_V_COLS)]
    out_shape = [
        jax.ShapeDtypeStruct((b, A_Q_COLS, s), BF16),
        jax.ShapeDtypeStruct((b, keys, A_KV_COLS), BF16),
        jax.ShapeDtypeStruct((b, A_KV_COLS, keys), BF16),
        jax.ShapeDtypeStruct((b, B_QK_COLS, s), BF16),
        jax.ShapeDtypeStruct((b, keys, B_QK_COLS), BF16),
        jax.ShapeDtypeStruct((b, B_V_COLS, keys), BF16),
    ]
    if emit_cache:
        assert not with_past
        out_specs += [tk(A_KV_COLS), tk(A_KV_COLS), tk(B_QK_COLS), tk(B_V_COLS)]
        out_shape += [
            jax.ShapeDtypeStruct((b, s, A_KV_COLS), F32),
            jax.ShapeDtypeStruct((b, s, A_KV_COLS), F32),
            jax.ShapeDtypeStruct((b, s, B_QK_COLS), F32),
            jax.ShapeDtypeStruct((b, s, B_V_COLS), F32),
        ]
    return pl.pallas_call(
        functools.partial(_proj_kernel, use_rope=use_rope, emit_cache=emit_cache,
                          with_past=with_past),
        grid=(b // rows, s // tm + lead),
        in_specs=in_specs,
        out_specs=out_specs,
        out_shape=out_shape,
        compiler_params=pltpu.CompilerParams(
            dimension_semantics=("arbitrary", "arbitrary"), vmem_limit_bytes=VMEM_LIMIT),
        name="projection_rope" if use_rope else "projection_ctx",
    )(*args)


ONES_ROWS = 16
ATTN_UNITS = A_Q_HEADS + 2 * B_HEADS
ATTN_UNROLL = 8


def _attn_kernel(qa_ref, qb_ref, ka_ref, va_ref, kb_ref, vb_ref, lamv_ref, gs_ref,
                 oa_ref, ob_ref, qpad_sc, m_sc, acc_sc, s_sc, bm_sc, *, tq, tk, n_kv, unroll, n_tiles):
    zeros = jnp.zeros((HEAD_DIM, tq), BF16)
    ones = jnp.ones((ONES_ROWS, tk), BF16)

    units = []
    for hd in range(A_Q_HEADS):
        g = hd // A_GROUP
        units.append((qa_ref, slice(hd * HEAD_DIM, (hd + 1) * HEAD_DIM), g,
                      ka_ref, slice(None), va_ref, slice(g * HEAD_DIM, (g + 1) * HEAD_DIM), HEAD_DIM))
    for hd in range(B_HEADS):
        base = hd * B_V_DIM
        for c in range(2):
            units.append((qb_ref, slice(base + c * HEAD_DIM, base + (c + 1) * HEAD_DIM), c,
                          kb_ref, slice(base, base + B_V_DIM), vb_ref, slice(base, base + B_V_DIM),
                          B_V_DIM))
    first_b_unit = A_Q_HEADS
    all_units = range(len(units))

    def tile_lanes(t):
        if isinstance(t, int):
            return pl.ds(t * tq, tq)
        return pl.ds(pl.multiple_of(t * tq, tq), tq)

    def load_queries(t, qslot):
        for u in all_units:
            q_ref, q_rows, half = units[u][:3]
            pieces = [zeros, zeros]
            pieces[half] = q_ref[0, q_rows, tile_lanes(t)]
            qpad_sc[qslot, u] = jnp.concatenate(pieces, axis=0)

    def block_offset(j):
        if isinstance(j, int):
            return j * tk
        return pl.multiple_of(j * tk, tk)

    def scores(j, slot, u, qslot):
        k_ref, k_lanes = units[u][3:5]
        kb = k_ref[0, pl.ds(block_offset(j), tk), k_lanes]
        s = jnp.dot(kb, qpad_sc[qslot, u], preferred_element_type=F32)
        s_sc[slot, u] = s
        bm_sc[slot, u] = jnp.max(s, axis=0, keepdims=True)

    def finish(j, slot, u, first=False):
        v_ref, v_rows, dv = units[u][5:]
        vext = jnp.concatenate([v_ref[0, v_rows, pl.ds(block_offset(j), tk)], ones], axis=0)
        rows = dv + ONES_ROWS
        if first:
            m_new = bm_sc[slot, u]
        else:
            m_prev = m_sc[u]
            m_new = jnp.maximum(m_prev, bm_sc[slot, u])
        p = jnp.exp2(s_sc[slot, u] - m_new).astype(BF16)
        pv = jnp.dot(vext, p, preferred_element_type=F32)
        if first:
            acc_sc[u, :rows] = pv
        else:
            acc_sc[u, :rows] = jnp.exp2(m_prev - m_new) * acc_sc[u, :rows] + pv
        m_sc[u] = m_new

    assert unroll % 2 == 0
    last = n_kv - 1
    assert n_tiles == 1 or last % 2 == 1, "next tile's block 0 reuses slot 0"

    def advance(j, slot, qslot, first=False):
        for u in all_units:
            scores(j + 1, 1 - slot, u, qslot)
            finish(j, slot, u, first)

    def first_block(qslot):
        if n_kv > 1:
            advance(0, 0, qslot, first=True)

    def middle_blocks(qslot):
        def body(i, carry):
            for k in range(unroll):
                advance(1 + unroll * i + k, (1 + k) % 2, qslot)
            return carry

        n_iter = max(n_kv - 2, 0) // unroll
        if n_iter > 0:
            lax.fori_loop(0, n_iter, body, 0)
        for j in range(1 + unroll * n_iter, last):
            advance(j, j % 2, qslot)

    lv = lamv_ref[...]
    lam = (jnp.exp(jnp.sum(lv[0:1] * lv[1:2], axis=-1, keepdims=True))
           - jnp.exp(jnp.sum(lv[2:3] * lv[3:4], axis=-1, keepdims=True)) + LAM_INIT)
    gain = gs_ref[...] * (1.0 - LAM_INIT)

    def result(u):
        dv = units[u][7]
        return acc_sc[u, :dv] * (1.0 / acc_sc[u, dv:dv + 1])

    def store_tile(t):
        for hd in range(A_Q_HEADS):
            oa_ref[0, hd * HEAD_DIM:(hd + 1) * HEAD_DIM, tile_lanes(t)] = result(hd).astype(BF16)
        for hd in range(B_HEADS):
            base = hd * B_V_DIM
            u = first_b_unit + 2 * hd
            d = result(u) - lam * result(u + 1)
            ob_ref[0, base:base + B_V_DIM, tile_lanes(t)] = (_rms(d, 0) * gain).astype(BF16)

    load_queries(0, 0)
    for u in all_units:
        scores(0, 0, u, 0)
    first_block(0)

    def tile_body(t, carry):
        qslot = jnp.bitwise_and(t, 1)
        middle_blocks(qslot)
        load_queries(t + 1, 1 - qslot)
        for u in all_units:
            scores(0, 0, u, 1 - qslot)
            finish(last, last % 2, u)
        store_tile(t)
        first_block(1 - qslot)
        return carry

    if n_tiles > 1:
        lax.fori_loop(0, n_tiles - 1, tile_body, 0)
    middle_blocks((n_tiles - 1) % 2)
    for u in all_units:
        finish(last, last % 2, u, first=(n_kv == 1))
    store_tile(n_tiles - 1)


def _attention(qa, qb, ka, va, kb, vb, lamv, gs, tq, tk, n_tiles):
    b, _, s = qa.shape
    t = ka.shape[1]
    qspec = pl.BlockSpec((1, A_Q_COLS, n_tiles * tq), lambda bi, i: (bi, 0, i))
    whole = lambda a: pl.BlockSpec((1,) + a.shape[1:], lambda bi, i: (bi, 0, 0))
    full = lambda a: pl.BlockSpec(a.shape, lambda bi, i: (0,) * a.ndim)
    return pl.pallas_call(
        functools.partial(_attn_kernel, tq=tq, tk=tk, n_kv=t // tk, unroll=ATTN_UNROLL,
                          n_tiles=n_tiles),
        grid=(b, s // (n_tiles * tq)),
        in_specs=[qspec, qspec, whole(ka), whole(va), whole(kb), whole(vb), full(lamv), full(gs)],
        out_specs=[qspec, qspec],
        out_shape=[jax.ShapeDtypeStruct((b, A_Q_COLS, s), BF16),
                   jax.ShapeDtypeStruct((b, B_V_COLS, s), BF16)],
        scratch_shapes=[
            pltpu.VMEM((2, ATTN_UNITS, 2 * HEAD_DIM, tq), BF16),
            pltpu.VMEM((ATTN_UNITS, 1, tq), F32),
            pltpu.VMEM((ATTN_UNITS, B_V_DIM + ONES_ROWS, tq), F32),
            pltpu.VMEM((2, ATTN_UNITS, tk, tq), F32),
            pltpu.VMEM((2, ATTN_UNITS, 1, tq), F32),
        ],
        compiler_params=pltpu.CompilerParams(
            dimension_semantics=("arbitrary", "arbitrary"), vmem_limit_bytes=VMEM_LIMIT),
        name="attention_t%d" % t,
    )(qa, qb, ka, va, kb, vb, lamv, gs)


def _out_kernel(x_ref, mods_ref, g1_ref, g2_ref, gf_ref, oa_ref, ob_ref, wg_ref, bg_ref,
                wa_ref, wb_ref, wo_ref, w1_ref, w2_ref, y_ref):
    n_rows, tm, _ = x_ref.shape
    if n_rows > 1:
        subs = [(r, slice(None)) for r in range(n_rows)]
    elif tm % (2 * PROJ_MIN_SUBTILE) == 0:
        subs = [(0, slice(k * (tm // 2), (k + 1) * (tm // 2))) for k in range(2)]
    else:
        subs = [(0, slice(None))]
    mods = mods_ref[0]
    tn = (((0,), (0,)), ((), ()))
    dot = functools.partial(jnp.dot, preferred_element_type=F32)
    xs = [x_ref[r, tok, :] for r, tok in subs]
    hs = [_modulated_norm(x, g1_ref[...], mods[1:2], mods[0:1]).astype(BF16) for x in xs]
    gates = [jax.nn.sigmoid(dot(h, wg_ref[...]) + bg_ref[...]) for h in hs]
    br_a = [lax.dot_general(oa_ref[r, :, tok], wa_ref[...], tn, preferred_element_type=F32)
            for r, tok in subs]
    br_b = [lax.dot_general(ob_ref[r, :, tok], wb_ref[...], tn, preferred_element_type=F32)
            for r, tok in subs]
    merged = [(g[:, :D_MODEL] * a + g[:, D_MODEL:] * b).astype(BF16)
              for g, a, b in zip(gates, br_a, br_b)]
    xs = [x + mods[2:3] * dot(m, wo_ref[...]) for x, m in zip(xs, merged)]
    h2 = [_modulated_norm(x, g2_ref[...], mods[4:5], mods[3:4]).astype(BF16) for x in xs]
    f = [jnp.square(jnp.maximum(dot(h, w1_ref[...]), 0.0)).astype(BF16) for h in h2]
    xs = [x + mods[5:6] * dot(a, w2_ref[...]) for x, a in zip(xs, f)]
    for (r, tok), x in zip(subs, xs):
        y_ref[r, tok, :] = _rms(x, -1) * gf_ref[...]


def _output(x, mods, mod_row, g1, g2, gf, oa, ob, wg, bg, wa, wb, wo, w1, w2, tm, rows=1):
    b, s, _ = x.shape
    full = lambda a: pl.BlockSpec(a.shape, lambda bi, i: (0,) * a.ndim)
    xspec = pl.BlockSpec((rows, tm, D_MODEL), lambda bi, i: (bi, i, 0))
    ospec = pl.BlockSpec((rows, A_Q_COLS, tm), lambda bi, i: (bi, 0, i))
    return pl.pallas_call(
        _out_kernel,
        grid=(b // rows, s // tm),
        in_specs=[xspec,
                  pl.BlockSpec((1, N_MOD, D_MODEL), lambda bi, i: (mod_row(bi * rows), 0, 0)),
                  full(g1), full(g2), full(gf), ospec, ospec,
                  full(wg), full(bg), full(wa), full(wb), full(wo), full(w1), full(w2)],
        out_specs=xspec,
        out_shape=jax.ShapeDtypeStruct(x.shape, F32),
        compiler_params=pltpu.CompilerParams(
            dimension_semantics=("arbitrary", "arbitrary"), vmem_limit_bytes=VMEM_LIMIT),
        name="output_s%d" % s,
    )(x, mods, g1, g2, gf, oa, ob, wg, bg, wa, wb, wo, w1, w2)


TOKEN_TILE = 512
ATTN_QUERY_TILE = PROJ_MIN_SUBTILE
ATTN_KEY_BLOCK = 256
ATTN_MAX_TILES = 8


class _Tiles(NamedTuple):
    tm: int
    tq: int
    tk: int
    n_tiles: int


def _tiles(n_tokens, n_keys):
    tm = min(n_tokens, TOKEN_TILE)
    tq = min(n_tokens, ATTN_QUERY_TILE)
    tk = min(n_keys, ATTN_KEY_BLOCK)
    assert n_tokens % tm == 0 and n_tokens % tq == 0 and n_keys % tk == 0
    q_tiles = n_tokens // tq
    n_tiles = 1
    if (n_keys // tk) % 2 == 0:
        n_tiles = max(d for d in range(1, ATTN_MAX_TILES + 1) if q_tiles % d == 0)
    return _Tiles(tm, tq, tk, n_tiles)


def _rope_tables_t(n_tokens):
    pos = jnp.arange(n_tokens, dtype=jnp.int32)
    row = (pos // GRID_W).astype(F32)
    col = (pos % GRID_W).astype(F32)
    n_freq = HEAD_DIM // 4
    freqs = ROPE_BASE ** (-jnp.arange(n_freq, dtype=F32) / n_freq)
    ang = jnp.concatenate([freqs[:, None] * row[None, :], freqs[:, None] * col[None, :]], axis=0)
    return jnp.cos(ang), jnp.sin(ang)


def kernel(x_prompt, x_sample, cache_a_k, cache_a_v, cache_b_k, cache_b_v, c, c_ctx, w_mod, b_mod, norm1_g, w_in, a_q_norm_g, a_k_norm_g, lam_q1, lam_k1, lam_q2, lam_k2, b_subln_g, w_gate, b_gate, w_br_a, w_br_b, w_out, norm2_g, w_fc1, w_fc2, final_norm_g):
    nb, seq, _ = x_prompt.shape
    db, dseq, _ = x_sample.shape
    past = cache_a_k.shape[2]
    l = 0

    cond = jnp.concatenate([c, c_ctx[None, :]], axis=0)
    ctx_row = db
    rows = -(-cond.shape[0] // 8) * 8
    cond = jnp.pad(cond, ((0, rows - cond.shape[0]), (0, 0)))
    mods = _modulation(cond, w_mod[l], b_mod[l]).reshape(rows, N_MOD, D_MODEL)

    g1 = norm1_g[l].reshape(1, D_MODEL)
    g2 = norm2_g[l].reshape(1, D_MODEL)
    gf = final_norm_g.reshape(1, D_MODEL)
    gq = a_q_norm_g[l].reshape(HEAD_DIM, 1)
    gk = a_k_norm_g[l].reshape(HEAD_DIM, 1)
    gs = b_subln_g[l].reshape(B_V_DIM, 1)
    lamv = jnp.stack([lam_q1[l], lam_k1[l], lam_q2[l], lam_k2[l]], axis=0)
    w_in_t = w_in[l].T.astype(BF16)
    wg = w_gate[l].astype(BF16)
    bg = b_gate[l].reshape(1, -1)
    wa = w_br_a[l].astype(BF16)
    wb = w_br_b[l].astype(BF16)
    wo = w_out[l].astype(BF16)
    w1 = w_fc1[l].astype(BF16)
    w2 = w_fc2[l].astype(BF16)

    ctx_mod = lambda bi: ctx_row
    lat_mod = lambda bi: bi

    tiles = _tiles(seq, seq)
    batch_rows = TOKEN_TILE // tiles.tm
    if nb % batch_rows:
        batch_rows = 1
    qa, ka, va, qb, kb, vb, new_ak, new_av, new_bk, new_bv = _projection(
        x_prompt, mods, ctx_mod, g1, w_in_t, gq, gk, None, True, tiles.tm, rows=batch_rows)
    oa, ob = _attention(qa, qb, ka, va, kb, vb, lamv, gs, tiles.tq, tiles.tk, tiles.n_tiles)
    y_prompt = _output(x_prompt, mods, ctx_mod, g1, g2, gf, oa, ob, wg, bg, wa, wb, wo, w1, w2,
                       tiles.tm, rows=batch_rows)

    tiles = _tiles(dseq, past + dseq)
    assert tiles.tm == past, "the cached context must fill exactly one projection tile"
    rope = _rope_tables_t(dseq)
    cached = (cache_a_k[:, l].reshape(db, past, A_KV_COLS), cache_a_v[:, l].reshape(db, past, A_KV_COLS),
              cache_b_k[:, l].reshape(db, past, B_QK_COLS), cache_b_v[:, l].reshape(db, past, B_V_COLS))
    qa, ka, va, qb, kb, vb = _projection(
        x_sample, mods, lat_mod, g1, w_in_t, gq, gk, rope, False, tiles.tm, past=cached)
    oa, ob = _attention(qa, qb, ka, va, kb, vb, lamv, gs, tiles.tq, tiles.tk, tiles.n_tiles)
    y_sample = _output(x_sample, mods, lat_mod, g1, g2, gf, oa, ob, wg, bg, wa, wb, wo, w1, w2,
                       tiles.tm)

    return (y_prompt, y_sample,
            new_ak.reshape(nb, 1, seq, A_KV_HEADS, HEAD_DIM),
            new_av.reshape(nb, 1, seq, A_KV_HEADS, HEAD_DIM),
            new_bk.reshape(nb, 1, seq, B_HEADS, 2, HEAD_DIM),
            new_bv.reshape(nb, 1, seq, B_HEADS, B_V_DIM))
```

```python
import functools
from typing import NamedTuple

import jax
import jax.numpy as jnp
from jax import lax
from jax.experimental import pallas as pl
from jax.experimental.pallas import tpu as pltpu

F32 = jnp.float32
BF16 = jnp.bfloat16

D_MODEL = 1024
HEAD_DIM = 64
HALF = HEAD_DIM // 2
GRID_W = 64
A_Q_HEADS = 8
A_KV_HEADS = 2
A_GROUP = A_Q_HEADS // A_KV_HEADS
B_HEADS = 4
B_V_DIM = 2 * HEAD_DIM
A_Q_COLS = A_Q_HEADS * HEAD_DIM
A_KV_COLS = A_KV_HEADS * HEAD_DIM
B_QK_COLS = B_HEADS * 2 * HEAD_DIM
B_V_COLS = B_HEADS * B_V_DIM
OFF_AQ = 0
OFF_AK = OFF_AQ + A_Q_COLS
OFF_AV = OFF_AK + A_KV_COLS
OFF_BQ = OFF_AV + A_KV_COLS
OFF_BK = OFF_BQ + B_QK_COLS
OFF_BV = OFF_BK + B_QK_COLS
IN_COLS = OFF_BV + B_V_COLS
D_FF = 4 * D_MODEL
N_MOD = 6
ROPE_BASE = 10000.0
EPS = 1e-6
LAM_INIT = 0.2
LOG2E = 1.4426950408889634
Q_PRESCALE = HEAD_DIM ** -0.5 * LOG2E

V7X_VMEM_BYTES = 64 * 1024 * 1024
VMEM_LIMIT = V7X_VMEM_BYTES - 8 * 1024 * 1024


def _rms(x, axis):
    return x * lax.rsqrt(jnp.mean(x * x, axis=axis, keepdims=True) + EPS)


def _modulated_norm(x, g, scale, shift):
    return (_rms(x, -1) * g) * (1.0 + scale) + shift


def _mod_kernel(cond_ref, w_ref, b_ref, o_ref):
    c = cond_ref[...]
    s = c * jax.nn.sigmoid(c)
    o_ref[...] = jnp.dot(s.astype(BF16), w_ref[...].astype(BF16),
                         preferred_element_type=F32) + b_ref[...]


def _modulation(cond, w_mod, b_mod):
    rows = cond.shape[0]
    n = w_mod.shape[1]
    tn = 1536
    return pl.pallas_call(
        _mod_kernel,
        grid=(n // tn,),
        in_specs=[
            pl.BlockSpec((rows, D_MODEL), lambda j: (0, 0)),
            pl.BlockSpec((D_MODEL, tn), lambda j: (0, j)),
            pl.BlockSpec((1, tn), lambda j: (0, j)),
        ],
        out_specs=pl.BlockSpec((rows, tn), lambda j: (0, j)),
        out_shape=jax.ShapeDtypeStruct((rows, n), F32),
        compiler_params=pltpu.CompilerParams(
            dimension_semantics=("arbitrary",), vmem_limit_bytes=VMEM_LIMIT),
        name="modulation",
    )(cond, w_mod, b_mod.reshape(1, n))


PROJ_MIN_SUBTILE = 256
PROJ_ROW_CHUNKS = ((OFF_AQ, OFF_AK), (OFF_AK, OFF_BK), (OFF_BK, OFF_BV), (OFF_BV, IN_COLS))


def _rope_t(x, cos, sin):
    x1, x2 = x[:HALF], x[HALF:]
    return jnp.concatenate([x1 * cos - x2 * sin, x1 * sin + x2 * cos], axis=0)


def _proj_kernel(*refs, use_rope, emit_cache, with_past):
    it = iter(refs)
    x_ref, mods_ref, g1_ref, w_ref, gq_ref, gk_ref = (next(it) for _ in range(6))
    cos_ref = sin_ref = None
    if use_rope:
        cos_ref, sin_ref = next(it), next(it)
    if with_past:
        pak_ref, pav_ref, pbk_ref, pbv_ref = (next(it) for _ in range(4))
    qa_ref, ka_ref, va_ref, qb_ref, kb_ref, vb_ref = (next(it) for _ in range(6))
    if emit_cache:
        cak_ref, cav_ref, cbk_ref, cbv_ref = (next(it) for _ in range(4))

    n_rows, tm, _ = x_ref.shape
    if n_rows > 1:
        subs = [(r, slice(None)) for r in range(n_rows)]
    elif tm % (2 * PROJ_MIN_SUBTILE) == 0:
        subs = [(0, slice(k * (tm // 2), (k + 1) * (tm // 2))) for k in range(2)]
    else:
        subs = [(0, slice(None))]

    def project():
        mods = mods_ref[0]
        g1 = g1_ref[...]
        nt = (((1,), (1,)), ((), ()))
        t = []
        for b, cols in subs:
            x = x_ref[b, cols, :]
            h = _modulated_norm(x, g1, mods[1:2], mods[0:1]).astype(BF16)
            t.append([lax.dot_general(w_ref[r0:r1, :], h, nt, preferred_element_type=F32)
                      for r0, r1 in PROJ_ROW_CHUNKS])
        gq, gk = gq_ref[...], gk_ref[...]

        for s, (b, cols) in enumerate(subs):
            if use_rope:
                cos, sin = cos_ref[:, cols], sin_ref[:, cols]

            def rows(r0, r1):
                for c, (c0, c1) in enumerate(PROJ_ROW_CHUNKS):
                    if c0 <= r0 and r1 <= c1:
                        return t[s][c][r0 - c0:r1 - c0]
                raise ValueError("row range crosses a projection chunk")

            def head(off, gain=None):
                blk = rows(off, off + HEAD_DIM)
                if gain is not None:
                    blk = _rms(blk, 0) * gain
                if use_rope:
                    blk = _rope_t(blk, cos, sin)
                return blk

            for i in range(A_Q_HEADS):
                off = OFF_AQ + i * HEAD_DIM
                qa_ref[b, i * HEAD_DIM:(i + 1) * HEAD_DIM, cols] = (
                    head(off, gq) * Q_PRESCALE).astype(BF16)
            ak_t = jnp.concatenate(
                [head(OFF_AK + i * HEAD_DIM, gk) for i in range(A_KV_HEADS)], axis=0)
            ak = ak_t.T
            ka_ref[b, cols, :] = ak.astype(BF16)
            av_t = rows(OFF_AV, OFF_AV + A_KV_COLS)
            va_ref[b, :, cols] = av_t.astype(BF16)
            for i in range(2 * B_HEADS):
                off = OFF_BQ + i * HEAD_DIM
                qb_ref[b, i * HEAD_DIM:(i + 1) * HEAD_DIM, cols] = (
                    head(off) * Q_PRESCALE).astype(BF16)
            bk_t = jnp.concatenate(
                [head(OFF_BK + i * HEAD_DIM) for i in range(2 * B_HEADS)], axis=0)
            bk = bk_t.T
            kb_ref[b, cols, :] = bk.astype(BF16)
            bv_t = rows(OFF_BV, OFF_BV + B_V_COLS)
            vb_ref[b, :, cols] = bv_t.astype(BF16)
            if emit_cache:
                cak_ref[b, cols, :] = ak
                cbk_ref[b, cols, :] = bk
                cav_ref[b, cols, :] = av_t.T
                cbv_ref[b, cols, :] = bv_t.T

    if not with_past:
        project()
        return

    step = pl.program_id(1)

    @pl.when(step == 0)
    def _():
        ka_ref[0] = pak_ref[0].astype(BF16)
        kb_ref[0] = pbk_ref[0].astype(BF16)
        va_ref[0] = pav_ref[0].T.astype(BF16)
        vb_ref[0] = pbv_ref[0].T.astype(BF16)

    pl.when(step > 0)(project)


def _projection(x, mods, mod_row, g1, w_in_t, gq, gk, rope, emit_cache, tm, past=None, rows=1):
    b, s, _ = x.shape
    use_rope = rope is not None
    with_past = past is not None
    lead = 1 if with_past else 0
    if with_past:
        assert rows == 1
        assert all(p.shape[1] == tm for p in past), "cached context must fill exactly one key block"
    tok = lambda i: jnp.maximum(i - lead, 0)
    full = lambda shape: pl.BlockSpec(shape, lambda bi, i: (0,) * len(shape))
    in_specs = [
        pl.BlockSpec((rows, tm, D_MODEL), lambda bi, i: (bi, tok(i), 0)),
        pl.BlockSpec((1, N_MOD, D_MODEL), lambda bi, i: (mod_row(bi * rows), 0, 0)),
        full((1, D_MODEL)),
        full((IN_COLS, D_MODEL)),
        full((HEAD_DIM, 1)),
        full((HEAD_DIM, 1)),
    ]
    args = [x, mods, g1, w_in_t, gq, gk]
    if use_rope:
        in_specs += [pl.BlockSpec((HALF, tm), lambda bi, i: (0, tok(i)))] * 2
        args += list(rope)
    if with_past:
        in_specs += [pl.BlockSpec((1,) + p.shape[1:], lambda bi, i: (bi, 0, 0)) for p in past]
        args += list(past)
    keys = s + lead * tm
    q_spec = lambda feat: pl.BlockSpec((rows, feat, tm), lambda bi, i: (bi, 0, tok(i)))
    fm = lambda feat: pl.BlockSpec((rows, feat, tm), lambda bi, i: (bi, 0, i))
    tk = lambda cols: pl.BlockSpec((rows, tm, cols), lambda bi, i: (bi, i, 0))
    out_specs = [q_spec(A_Q_COLS), tk(A_KV_COLS), fm(A_KV_COLS),
                 q_spec(B_QK_COLS), tk(B_QK_COLS), fm(B_V_COLS)]
    out_shape = [
        jax.ShapeDtypeStruct((b, A_Q_COLS, s), BF16),
        jax.ShapeDtypeStruct((b, keys, A_KV_COLS), BF16),
        jax.ShapeDtypeStruct((b, A_KV_COLS, keys), BF16),
        jax.ShapeDtypeStruct((b, B_QK_COLS, s), BF16),
        jax.ShapeDtypeStruct((b, keys, B_QK_COLS), BF16),
        jax.ShapeDtypeStruct((b, B_V_COLS, keys), BF16),
    ]
    if emit_cache:
        assert not with_past
        out_specs += [tk(A_KV_COLS), tk(A_KV_COLS), tk(B_QK_COLS), tk(B_V_COLS)]
        out_shape += [
            jax.ShapeDtypeStruct((b, s, A_KV_COLS), F32),
            jax.ShapeDtypeStruct((b, s, A_KV_COLS), F32),
            jax.ShapeDtypeStruct((b, s, B_QK_COLS), F32),
            jax.ShapeDtypeStruct((b, s, B_V_COLS), F32),
        ]
    return pl.pallas_call(
        functools.partial(_proj_kernel, use_rope=use_rope, emit_cache=emit_cache,
                          with_past=with_past),
        grid=(b // rows, s // tm + lead),
        in_specs=in_specs,
        out_specs=out_specs,
        out_shape=out_shape,
        compiler_params=pltpu.CompilerParams(
            dimension_semantics=("arbitrary", "arbitrary"), vmem_limit_bytes=VMEM_LIMIT),
        name="projection_rope" if use_rope else "projection_ctx",
    )(*args)


ONES_ROWS = 16
ATTN_UNITS = A_Q_HEADS + 2 * B_HEADS
ATTN_UNROLL = 8


def _attn_kernel(qa_ref, qb_ref, ka_ref, va_ref, kb_ref, vb_ref, lamv_ref, gs_ref,
                 oa_ref, ob_ref, qpad_sc, m_sc, acc_sc, s_sc, bm_sc, *, tq, tk, n_kv, unroll, n_tiles):
    zeros = jnp.zeros((HEAD_DIM, tq), BF16)
    ones = jnp.ones((ONES_ROWS, tk), BF16)

    n_rows = qa_ref.shape[0]
    units = []
    unit_row = []
    for r in range(n_rows):
        for hd in range(A_Q_HEADS):
            g = hd // A_GROUP
            units.append((qa_ref, slice(hd * HEAD_DIM, (hd + 1) * HEAD_DIM), g, ka_ref, slice(None),
                          va_ref, slice(g * HEAD_DIM, (g + 1) * HEAD_DIM), HEAD_DIM))
        for hd in range(B_HEADS):
            base = hd * B_V_DIM
            for c in range(2):
                units.append((qb_ref, slice(base + c * HEAD_DIM, base + (c + 1) * HEAD_DIM), c,
                              kb_ref, slice(base, base + B_V_DIM), vb_ref,
                              slice(base, base + B_V_DIM), B_V_DIM))
        unit_row += [r] * ATTN_UNITS
    first_b_unit = A_Q_HEADS
    all_units = range(len(units))

    def tile_lanes(t):
        if isinstance(t, int):
            return pl.ds(t * tq, tq)
        return pl.ds(pl.multiple_of(t * tq, tq), tq)

    def load_queries(t, qslot):
        for u in all_units:
            q_ref, q_rows, half = units[u][:3]
            pieces = [zeros, zeros]
            pieces[half] = q_ref[unit_row[u], q_rows, tile_lanes(t)]
            qpad_sc[qslot, u] = jnp.concatenate(pieces, axis=0)

    def block_offset(j):
        if isinstance(j, int):
            return j * tk
        return pl.multiple_of(j * tk, tk)

    def scores(j, slot, u, qslot):
        k_ref, k_lanes = units[u][3:5]
        kb = k_ref[unit_row[u], pl.ds(block_offset(j), tk), k_lanes]
        s = jnp.dot(kb, qpad_sc[qslot, u], preferred_element_type=F32)
        s_sc[slot, u] = s
        bm_sc[slot, u] = jnp.max(s, axis=0, keepdims=True)

    def finish(j, slot, u, first=False):
        v_ref, v_rows, dv = units[u][5:]
        vext = jnp.concatenate(
            [v_ref[unit_row[u], v_rows, pl.ds(block_offset(j), tk)], ones], axis=0)
        rows = dv + ONES_ROWS
        if first:
            m_new = bm_sc[slot, u]
        else:
            m_prev = m_sc[u]
            m_new = jnp.maximum(m_prev, bm_sc[slot, u])
        p = jnp.exp2(s_sc[slot, u] - m_new).astype(BF16)
        pv = jnp.dot(vext, p, preferred_element_type=F32)
        if first:
            acc_sc[u, :rows] = pv
        else:
            acc_sc[u, :rows] = jnp.exp2(m_prev - m_new) * acc_sc[u, :rows] + pv
        m_sc[u] = m_new

    assert unroll % 2 == 0
    last = n_kv - 1
    assert n_tiles == 1 or last % 2 == 1, "next tile's block 0 reuses slot 0"

    def advance(j, slot, qslot, first=False):
        for u in all_units:
            scores(j + 1, 1 - slot, u, qslot)
            finish(j, slot, u, first)

    def first_block(qslot):
        if n_kv > 1:
            advance(0, 0, qslot, first=True)

    def middle_blocks(qslot):
        def body(i, carry):
            for k in range(unroll):
                advance(1 + unroll * i + k, (1 + k) % 2, qslot)
            return carry

        n_iter = max(n_kv - 2, 0) // unroll
        if n_iter > 0:
            lax.fori_loop(0, n_iter, body, 0)
        for j in range(1 + unroll * n_iter, last):
            advance(j, j % 2, qslot)

    lv = lamv_ref[...]
    lam = (jnp.exp(jnp.sum(lv[0:1] * lv[1:2], axis=-1, keepdims=True))
           - jnp.exp(jnp.sum(lv[2:3] * lv[3:4], axis=-1, keepdims=True)) + LAM_INIT)
    gain = gs_ref[...] * (1.0 - LAM_INIT)

    def result(u):
        dv = units[u][7]
        return acc_sc[u, :dv] * (1.0 / acc_sc[u, dv:dv + 1])

    def store_tile(t):
        for r in range(n_rows):
            u0 = r * ATTN_UNITS
            for hd in range(A_Q_HEADS):
                oa_ref[r, hd * HEAD_DIM:(hd + 1) * HEAD_DIM, tile_lanes(t)] = (
                    result(u0 + hd).astype(BF16))
            for hd in range(B_HEADS):
                base = hd * B_V_DIM
                u = u0 + first_b_unit + 2 * hd
                d = result(u) - lam * result(u + 1)
                ob_ref[r, base:base + B_V_DIM, tile_lanes(t)] = (_rms(d, 0) * gain).astype(BF16)

    load_queries(0, 0)
    for u in all_units:
        scores(0, 0, u, 0)
    first_block(0)

    def tile_body(t, carry):
        qslot = jnp.bitwise_and(t, 1)
        middle_blocks(qslot)
        load_queries(t + 1, 1 - qslot)
        for u in all_units:
            scores(0, 0, u, 1 - qslot)
            finish(last, last % 2, u)
        store_tile(t)
        first_block(1 - qslot)
        return carry

    if n_tiles > 1:
        lax.fori_loop(0, n_tiles - 1, tile_body, 0)
    middle_blocks((n_tiles - 1) % 2)
    for u in all_units:
        finish(last, last % 2, u, first=(n_kv == 1))
    store_tile(n_tiles - 1)


def _attention(qa, qb, ka, va, kb, vb, lamv, gs, tq, tk, n_tiles, rows=1):
    b, _, s = qa.shape
    t = ka.shape[1]
    n_units = rows * ATTN_UNITS
    qspec = pl.BlockSpec((rows, A_Q_COLS, n_tiles * tq), lambda bi, i: (bi, 0, i))
    whole = lambda a: pl.BlockSpec((rows,) + a.shape[1:], lambda bi, i: (bi, 0, 0))
    full = lambda a: pl.BlockSpec(a.shape, lambda bi, i: (0,) * a.ndim)
    return pl.pallas_call(
        functools.partial(_attn_kernel, tq=tq, tk=tk, n_kv=t // tk, unroll=ATTN_UNROLL,
                          n_tiles=n_tiles),
        grid=(b // rows, s // (n_tiles * tq)),
        in_specs=[qspec, qspec, whole(ka), whole(va), whole(kb), whole(vb), full(lamv), full(gs)],
        out_specs=[qspec, qspec],
        out_shape=[jax.ShapeDtypeStruct((b, A_Q_COLS, s), BF16),
                   jax.ShapeDtypeStruct((b, B_V_COLS, s), BF16)],
        scratch_shapes=[
            pltpu.VMEM((2, n_units, 2 * HEAD_DIM, tq), BF16),
            pltpu.VMEM((n_units, 1, tq), F32),
            pltpu.VMEM((n_units, B_V_DIM + ONES_ROWS, tq), F32),
            pltpu.VMEM((2, n_units, tk, tq), F32),
            pltpu.VMEM((2, n_units, 1, tq), F32),
        ],
        compiler_params=pltpu.CompilerParams(
            dimension_semantics=("arbitrary", "arbitrary"), vmem_limit_bytes=VMEM_LIMIT),
        name="attention_t%d" % t,
    )(qa, qb, ka, va, kb, vb, lamv, gs)


def _out_kernel(x_ref, mods_ref, g1_ref, g2_ref, gf_ref, oa_ref, ob_ref, wg_ref, bg_ref,
                wa_ref, wb_ref, wo_ref, w1_ref, w2_ref, y_ref):
    n_rows, tm, _ = x_ref.shape
    if n_rows > 1:
        subs = [(r, slice(None)) for r in range(n_rows)]
    elif tm % (2 * PROJ_MIN_SUBTILE) == 0:
        subs = [(0, slice(k * (tm // 2), (k + 1) * (tm // 2))) for k in range(2)]
    else:
        subs = [(0, slice(None))]
    mods = mods_ref[0]
    tn = (((0,), (0,)), ((), ()))
    dot = functools.partial(jnp.dot, preferred_element_type=F32)
    xs = [x_ref[r, tok, :] for r, tok in subs]
    hs = [_modulated_norm(x, g1_ref[...], mods[1:2], mods[0:1]).astype(BF16) for x in xs]
    gates = [jax.nn.sigmoid(dot(h, wg_ref[...]) + bg_ref[...]) for h in hs]
    br_a = [lax.dot_general(oa_ref[r, :, tok], wa_ref[...], tn, preferred_element_type=F32)
            for r, tok in subs]
    br_b = [lax.dot_general(ob_ref[r, :, tok], wb_ref[...], tn, preferred_element_type=F32)
            for r, tok in subs]
    merged = [(g[:, :D_MODEL] * a + g[:, D_MODEL:] * b).astype(BF16)
              for g, a, b in zip(gates, br_a, br_b)]
    xs = [x + mods[2:3] * dot(m, wo_ref[...]) for x, m in zip(xs, merged)]
    h2 = [_modulated_norm(x, g2_ref[...], mods[4:5], mods[3:4]).astype(BF16) for x in xs]
    f = [jnp.square(jnp.maximum(dot(h, w1_ref[...]), 0.0)).astype(BF16) for h in h2]
    xs = [x + mods[5:6] * dot(a, w2_ref[...]) for x, a in zip(xs, f)]
    for (r, tok), x in zip(subs, xs):
        y_ref[r, tok, :] = _rms(x, -1) * gf_ref[...]


def _output(x, mods, mod_row, g1, g2, gf, oa, ob, wg, bg, wa, wb, wo, w1, w2, tm, rows=1):
    b, s, _ = x.shape
    full = lambda a: pl.BlockSpec(a.shape, lambda bi, i: (0,) * a.ndim)
    xspec = pl.BlockSpec((rows, tm, D_MODEL), lambda bi, i: (bi, i, 0))
    ospec = pl.BlockSpec((rows, A_Q_COLS, tm), lambda bi, i: (bi, 0, i))
    return pl.pallas_call(
        _out_kernel,
        grid=(b // rows, s // tm),
        in_specs=[xspec,
                  pl.BlockSpec((1, N_MOD, D_MODEL), lambda bi, i: (mod_row(bi * rows), 0, 0)),
                  full(g1), full(g2), full(gf), ospec, ospec,
                  full(wg), full(bg), full(wa), full(wb), full(wo), full(w1), full(w2)],
        out_specs=xspec,
        out_shape=jax.ShapeDtypeStruct(x.shape, F32),
        compiler_params=pltpu.CompilerParams(
            dimension_semantics=("arbitrary", "arbitrary"), vmem_limit_bytes=VMEM_LIMIT),
        name="output_s%d" % s,
    )(x, mods, g1, g2, gf, oa, ob, wg, bg, wa, wb, wo, w1, w2)


TOKEN_TILE = 512
ATTN_QUERY_TILE = PROJ_MIN_SUBTILE
ATTN_KEY_BLOCK = 256
ATTN_MAX_TILES = 8


class _Tiles(NamedTuple):
    tm: int
    tq: int
    tk: int
    n_tiles: int


def _tiles(n_tokens, n_keys):
    tm = min(n_tokens, TOKEN_TILE)
    tq = min(n_tokens, ATTN_QUERY_TILE)
    tk = min(n_keys, ATTN_KEY_BLOCK)
    assert n_tokens % tm == 0 and n_tokens % tq == 0 and n_keys % tk == 0
    q_tiles = n_tokens // tq
    n_tiles = 1
    if (n_keys // tk) % 2 == 0:
        n_tiles = max(d for d in range(1, ATTN_MAX_TILES + 1) if q_tiles % d == 0)
    return _Tiles(tm, tq, tk, n_tiles)


def _rope_tables_t(n_tokens):
    pos = jnp.arange(n_tokens, dtype=jnp.int32)
    row = (pos // GRID_W).astype(F32)
    col = (pos % GRID_W).astype(F32)
    n_freq = HEAD_DIM // 4
    freqs = ROPE_BASE ** (-jnp.arange(n_freq, dtype=F32) / n_freq)
    ang = jnp.concatenate([freqs[:, None] * row[None, :], freqs[:, None] * col[None, :]], axis=0)
    return jnp.cos(ang), jnp.sin(ang)


def kernel(x_prompt, x_sample, cache_a_k, cache_a_v, cache_b_k, cache_b_v, c, c_ctx, w_mod, b_mod, norm1_g, w_in, a_q_norm_g, a_k_norm_g, lam_q1, lam_k1, lam_q2, lam_k2, b_subln_g, w_gate, b_gate, w_br_a, w_br_b, w_out, norm2_g, w_fc1, w_fc2, final_norm_g):
    nb, seq, _ = x_prompt.shape
    db, dseq, _ = x_sample.shape
    past = cache_a_k.shape[2]
    l = 0

    cond = jnp.concatenate([c, c_ctx[None, :]], axis=0)
    ctx_row = db
    rows = -(-cond.shape[0] // 8) * 8
    cond = jnp.pad(cond, ((0, rows - cond.shape[0]), (0, 0)))
    mods = _modulation(cond, w_mod[l], b_mod[l]).reshape(rows, N_MOD, D_MODEL)

    g1 = norm1_g[l].reshape(1, D_MODEL)
    g2 = norm2_g[l].reshape(1, D_MODEL)
    gf = final_norm_g.reshape(1, D_MODEL)
    gq = a_q_norm_g[l].reshape(HEAD_DIM, 1)
    gk = a_k_norm_g[l].reshape(HEAD_DIM, 1)
    gs = b_subln_g[l].reshape(B_V_DIM, 1)
    lamv = jnp.stack([lam_q1[l], lam_k1[l], lam_q2[l], lam_k2[l]], axis=0)
    w_in_t = w_in[l].T.astype(BF16)
    wg = w_gate[l].astype(BF16)
    bg = b_gate[l].reshape(1, -1)
    wa = w_br_a[l].astype(BF16)
    wb = w_br_b[l].astype(BF16)
    wo = w_out[l].astype(BF16)
    w1 = w_fc1[l].astype(BF16)
    w2 = w_fc2[l].astype(BF16)

    ctx_mod = lambda bi: ctx_row
    lat_mod = lambda bi: bi

    tiles = _tiles(seq, seq)
    batch_rows = TOKEN_TILE // tiles.tm
    if nb % batch_rows:
        batch_rows = 1
    qa, ka, va, qb, kb, vb, new_ak, new_av, new_bk, new_bv = _projection(
        x_prompt, mods, ctx_mod, g1, w_in_t, gq, gk, None, True, tiles.tm, rows=batch_rows)
    oa, ob = _attention(qa, qb, ka, va, kb, vb, lamv, gs, tiles.tq, tiles.tk, tiles.n_tiles,
                        rows=batch_rows)
    y_prompt = _output(x_prompt, mods, ctx_mod, g1, g2, gf, oa, ob, wg, bg, wa, wb, wo, w1, w2,
                       tiles.tm, rows=batch_rows)

    tiles = _tiles(dseq, past + dseq)
    assert tiles.tm == past, "the cached context must fill exactly one projection tile"
    rope = _rope_tables_t(dseq)
    cached = (cache_a_k[:, l].reshape(db, past, A_KV_COLS), cache_a_v[:, l].reshape(db, past, A_KV_COLS),
              cache_b_k[:, l].reshape(db, past, B_QK_COLS), cache_b_v[:, l].reshape(db, past, B_V_COLS))
    qa, ka, va, qb, kb, vb = _projection(
        x_sample, mods, lat_mod, g1, w_in_t, gq, gk, rope, False, tiles.tm, past=cached)
    oa, ob = _attention(qa, qb, ka, va, kb, vb, lamv, gs, tiles.tq, tiles.tk, tiles.n_tiles)
    y_sample = _output(x_sample, mods, lat_mod, g1, g2, gf, oa, ob, wg, bg, wa, wb, wo, w1, w2,
                       tiles.tm)

    return (y_prompt, y_sample,
            new_ak.reshape(nb, 1, seq, A_KV_HEADS, HEAD_DIM),
            new_av.reshape(nb, 1, seq, A_KV_HEADS, HEAD_DIM),
            new_bk.reshape(nb, 1, seq, B_HEADS, 2, HEAD_DIM),
            new_bv.reshape(nb, 1, seq, B_HEADS, B_V_DIM))
```
